```python
import jax, jax.numpy as jnp
from jax import lax
import numpy as np

D_MODEL = 1024
BATCH = 8
SEQ = 4096
DEPTH = 2

HEAD_DIM = 64
SB_WIDTH = D_MODEL // 2
SB_HEADS = SB_WIDTH // HEAD_DIM
LRU_WIDTH = D_MODEL // 4
LRU_BLOCKS = LRU_WIDTH // HEAD_DIM
XA_WIDTH = D_MODEL // 4
XA_HEADS = XA_WIDTH // HEAD_DIM
D_MIX = SB_WIDTH + LRU_WIDTH + XA_WIDTH
N_MEM = 256
CONV_WIDTH = 4
LRU_C = 8.0
Q_BLOCK = 128
EPS = 1e-6
IN_SPLITS = (SB_WIDTH, SB_WIDTH, SB_WIDTH, SB_WIDTH, LRU_WIDTH, LRU_WIDTH, XA_WIDTH, XA_WIDTH)
D_IN = 4 * SB_WIDTH + 2 * LRU_WIDTH + 2 * XA_WIDTH

kernel_name = 'hymba_style_stickbreak_rglru_memxattn'


def rmsnorm(x, g):
    xf = x.astype(jnp.float32)
    y = xf * lax.rsqrt(jnp.mean(xf * xf, axis=-1, keepdims=True) + EPS)
    return (y * g.astype(jnp.float32)).astype(x.dtype)


def stick_breaking_attention(q, k, v):
    S = q.shape[2]
    scale = HEAD_DIM ** -0.5
    qf, kf, vf = q.astype(jnp.float32), k.astype(jnp.float32), v.astype(jnp.float32)
    outs = []
    for start in range(0, S, Q_BLOCK):
        end = start + Q_BLOCK
        z = jnp.einsum('bhqd,bhkd->bhqk', qf[:, :, start:end], kf[:, :, :end]) * scale
        t_idx = start + jnp.arange(Q_BLOCK)[:, None]
        s_idx = jnp.arange(end)[None, :]
        mask = s_idx < t_idx
        log_fail = jnp.where(mask, -jax.nn.softplus(z), 0.0)
        suffix = lax.cumsum(log_fail, axis=log_fail.ndim - 1, reverse=True)
        later = jnp.pad(suffix[..., 1:], ((0, 0), (0, 0), (0, 0), (0, 1)))
        w = jnp.where(mask, jnp.exp(jax.nn.log_sigmoid(z) + later), 0.0)
        outs.append(jnp.einsum('bhqk,bhkd->bhqd', w, vf[:, :, :end]))
    return jnp.concatenate(outs, axis=2).astype(v.dtype)


def causal_depthwise_conv(x, w, b):
    C = x.shape[-1]
    y = lax.conv_general_dilated(x, w[:, None, :].astype(x.dtype), window_strides=(1,),
                                 padding=[(CONV_WIDTH - 1, 0)],
                                 dimension_numbers=('NWC', 'WIO', 'NWC'),
                                 feature_group_count=C)
    return y + b.astype(x.dtype)


def rg_lru(xc, w_rg, b_rg, w_ig, b_ig, lam):
    B, S, W = xc.shape
    xf = xc.astype(jnp.float32)
    xb = xf.reshape(B, S, LRU_BLOCKS, HEAD_DIM)
    r = jax.nn.sigmoid(jnp.einsum('bsnd,nde->bsne', xb, w_rg.astype(jnp.float32)).reshape(B, S, W)
                       + b_rg.astype(jnp.float32))
    i = jax.nn.sigmoid(jnp.einsum('bsnd,nde->bsne', xb, w_ig.astype(jnp.float32)).reshape(B, S, W)
                       + b_ig.astype(jnp.float32))
    log_a = -LRU_C * r * jax.nn.softplus(-lam.astype(jnp.float32))
    a = jnp.exp(log_a)
    u = jnp.sqrt(-jnp.expm1(2.0 * log_a)) * (i * xf)

    def combine(left, right):
        a_l, b_l = left
        a_r, b_r = right
        return a_l * a_r, a_r * b_l + b_r

    _, h = lax.associative_scan(combine, (a, u), axis=1)
    return h.astype(xc.dtype)


def memory_cross_attention(xq, mem, mem_g, w_mem_kv, q_g, k_g):
    B, S, _ = xq.shape
    M = mem.shape[1]
    kv = rmsnorm(mem, mem_g) @ w_mem_kv
    mk, mv = jnp.split(kv, 2, axis=-1)
    mk = rmsnorm(mk.reshape(B, M, XA_HEADS, HEAD_DIM), k_g)
    mv = mv.reshape(B, M, XA_HEADS, HEAD_DIM)
    q = rmsnorm(xq.reshape(B, S, XA_HEADS, HEAD_DIM), q_g)
    scores = jnp.einsum('bshd,bmhd->bhsm', q.astype(jnp.float32), mk.astype(jnp.float32)) * (HEAD_DIM ** -0.5)
    p = jax.nn.softmax(scores, axis=-1)
    o = jnp.einsum('bhsm,bmhd->bshd', p, mv.astype(jnp.float32))
    return o.reshape(B, S, XA_WIDTH).astype(xq.dtype)


def hybrid_layer(x, mem, norm_g, w_in, sb_q_g, sb_k_g, conv_w, conv_b, w_rg, b_rg, w_ig, b_ig,
                 lru_lambda, xa_q_g, xa_k_g, mem_g, w_mem_kv, w_out):
    B, S, _ = x.shape
    h = rmsnorm(x, norm_g)
    proj = h @ w_in
    sb_q, sb_k, sb_v, sb_gate, lru_x, lru_gate, xa_q, xa_gate = jnp.split(
        proj, [int(c) for c in np.cumsum(IN_SPLITS)[:-1]], axis=-1)

    def to_heads(t, n):
        return t.reshape(B, S, n, HEAD_DIM).transpose(0, 2, 1, 3)

    q = rmsnorm(to_heads(sb_q, SB_HEADS), sb_q_g)
    k = rmsnorm(to_heads(sb_k, SB_HEADS), sb_k_g)
    v = to_heads(sb_v, SB_HEADS)
    sb_out = stick_breaking_attention(q, k, v).transpose(0, 2, 1, 3).reshape(B, S, SB_WIDTH)

    xc = causal_depthwise_conv(lru_x, conv_w, conv_b)
    lru_out = rg_lru(xc, w_rg, b_rg, w_ig, b_ig, lru_lambda)

    xa_out = memory_cross_attention(xa_q, mem, mem_g, w_mem_kv, xa_q_g, xa_k_g)

    y = jnp.concatenate([sb_out * jax.nn.silu(sb_gate),
                         lru_out * jax.nn.silu(lru_gate),
                         xa_out * jax.nn.silu(xa_gate)], axis=-1)
    return x + (y @ w_out).astype(x.dtype)


def setup_inputs(seed: int = 0) -> dict:
    key = jax.random.key(seed)
    ks = jax.random.split(key, 20)
    f32 = jnp.float32

    def gain(k, shape):
        return 1.0 + 0.02 * jax.random.normal(k, shape, f32)

    u = jax.random.uniform(ks[12], (DEPTH, LRU_WIDTH), f32, minval=0.9, maxval=0.999)
    a0 = u ** (1.0 / LRU_C)
    return {
        'x': jax.random.normal(ks[0], (BATCH, SEQ, D_MODEL), f32),
        'mem': jax.random.normal(ks[1], (BATCH, N_MEM, D_MODEL), f32),
        'norm_g': gain(ks[2], (DEPTH, D_MODEL)),
        'w_in': jax.random.normal(ks[3], (DEPTH, D_MODEL, D_IN), f32) * D_MODEL ** -0.5,
        'sb_q_g': gain(ks[4], (DEPTH, HEAD_DIM)),
        'sb_k_g': gain(ks[5], (DEPTH, HEAD_DIM)),
        'conv_w': jax.random.normal(ks[6], (DEPTH, CONV_WIDTH, LRU_WIDTH), f32) * CONV_WIDTH ** -0.5,
        'conv_b': 0.01 * jax.random.normal(ks[7], (DEPTH, LRU_WIDTH), f32),
        'w_rg': jax.random.normal(ks[8], (DEPTH, LRU_BLOCKS, HEAD_DIM, HEAD_DIM), f32) * HEAD_DIM ** -0.5,
        'b_rg': 0.01 * jax.random.normal(ks[9], (DEPTH, LRU_WIDTH), f32),
        'w_ig': jax.random.normal(ks[10], (DEPTH, LRU_BLOCKS, HEAD_DIM, HEAD_DIM), f32) * HEAD_DIM ** -0.5,
        'b_ig': 0.01 * jax.random.normal(ks[11], (DEPTH, LRU_WIDTH), f32),
        'lru_lambda': jnp.log(a0) - jnp.log1p(-a0),
        'xa_q_g': gain(ks[13], (DEPTH, HEAD_DIM)),
        'xa_k_g': gain(ks[14], (DEPTH, HEAD_DIM)),
        'mem_g': gain(ks[15], (DEPTH, D_MODEL)),
        'w_mem_kv': jax.random.normal(ks[16], (DEPTH, D_MODEL, 2 * XA_WIDTH), f32) * D_MODEL ** -0.5,
        'w_out': jax.random.normal(ks[17], (DEPTH, D_MIX, D_MODEL), f32) * D_MIX ** -0.5,
    }


def reference(x, mem, norm_g, w_in, sb_q_g, sb_k_g, conv_w, conv_b, w_rg, b_rg, w_ig, b_ig,
              lru_lambda, xa_q_g, xa_k_g, mem_g, w_mem_kv, w_out):
    for l in range(DEPTH):
        x = hybrid_layer(x, mem, norm_g[l], w_in[l], sb_q_g[l], sb_k_g[l], conv_w[l], conv_b[l],
                         w_rg[l], b_rg[l], w_ig[l], b_ig[l], lru_lambda[l], xa_q_g[l], xa_k_g[l],
                         mem_g[l], w_mem_kv[l], w_out[l])
    return x
```

```python
import functools

import jax
import jax.numpy as jnp
from jax import lax
from jax.experimental import pallas as pl
from jax.experimental.pallas import tpu as pltpu

D_MODEL = 1024
HEAD_DIM = 64
SB_WIDTH = 512
LRU_WIDTH = 256
XA_WIDTH = 256
N_MEM = 256
CONV_WIDTH = 4
LRU_C = 8.0
EPS = 1e-6
D_IN = 4 * SB_WIDTH + 2 * LRU_WIDTH + 2 * XA_WIDTH
D_REST = D_IN - 3 * SB_WIDTH

MXU_WIDTH = 256
LANES = 128
SUBLANES = 8
ROW_TILE = 512
ATT_TILE = MXU_WIDTH
VMEM_LIMIT = 48 * 1024 * 1024

F32_EXP_ZERO = -104.0

F32 = jnp.float32
BF16 = jnp.bfloat16


def _dot(a, b):
    return jnp.dot(a, b, preferred_element_type=F32)


def _dot_nt(a, b):
    return lax.dot_general(a, b, (((1,), (1,)), ((), ())), preferred_element_type=F32)


def _head_rms(t, group_ones):
    outs = []
    for j in range(t.shape[1] // MXU_WIDTH):
        tj = t[:, j * MXU_WIDTH:(j + 1) * MXU_WIDTH]
        ss = _dot((tj * tj).astype(BF16), group_ones)
        outs.append(tj * lax.rsqrt(ss * (1.0 / HEAD_DIM) + EPS))
    return outs[0] if len(outs) == 1 else jnp.concatenate(outs, axis=1)


def _sigmoid(x):
    return 1.0 / (1.0 + jnp.exp(-x))


def _in_proj_kernel(x_ref, g_ref, w_ref, qg_ref, kg_ref, ones_ref,
                    q_ref, k_ref, v_ref, rest_ref):
    x = x_ref[...]
    ms = jnp.mean(x * x, axis=-1, keepdims=True)
    h = (x * lax.rsqrt(ms + EPS) * g_ref[...]).astype(BF16)
    ones = ones_ref[...]
    q = _dot(h, w_ref[:, 0:SB_WIDTH])
    q_ref[...] = (_head_rms(q, ones) * qg_ref[...]).astype(BF16)
    k = _dot(h, w_ref[:, SB_WIDTH:2 * SB_WIDTH])
    k_ref[...] = (_head_rms(k, ones) * kg_ref[...]).astype(BF16)
    v_ref[...] = _dot(h, w_ref[:, 2 * SB_WIDTH:3 * SB_WIDTH]).astype(BF16)
    for c in range(3 * SB_WIDTH, D_IN, SB_WIDTH):
        rest_ref[:, c - 3 * SB_WIDTH:c - 2 * SB_WIDTH] = _dot(h, w_ref[:, c:c + SB_WIDTH])


def _in_proj(x2d, norm_g, w_in, qg, kg, ones):
    m = x2d.shape[0]
    row = lambda i: (i, 0)
    const = lambda i: (0, 0)
    return pl.pallas_call(
        _in_proj_kernel,
        grid=(m // ROW_TILE,),
        in_specs=[
            pl.BlockSpec((ROW_TILE, D_MODEL), row),
            pl.BlockSpec((1, D_MODEL), const),
            pl.BlockSpec((D_MODEL, D_IN), const),
            pl.BlockSpec((1, SB_WIDTH), const),
            pl.BlockSpec((1, SB_WIDTH), const),
            pl.BlockSpec((MXU_WIDTH, MXU_WIDTH), const),
        ],
        out_specs=[
            pl.BlockSpec((ROW_TILE, SB_WIDTH), row),
            pl.BlockSpec((ROW_TILE, SB_WIDTH), row),
            pl.BlockSpec((ROW_TILE, SB_WIDTH), row),
            pl.BlockSpec((ROW_TILE, D_REST), row),
        ],
        out_shape=[
            jax.ShapeDtypeStruct((m, SB_WIDTH), BF16),
            jax.ShapeDtypeStruct((m, SB_WIDTH), BF16),
            jax.ShapeDtypeStruct((m, SB_WIDTH), BF16),
            jax.ShapeDtypeStruct((m, D_REST), F32),
        ],
        compiler_params=pltpu.CompilerParams(
            dimension_semantics=("arbitrary",), vmem_limit_bytes=VMEM_LIMIT),
        name="in_proj",
    )(x2d, norm_g, w_in, qg, kg, ones)


def _sb_attn_kernel(q_ref, k_ref, v_ref, upper_ref, o_ref, acc_ref):
    seq = q_ref.shape[0]
    n_tiles = seq // ATT_TILE
    lane = lax.broadcasted_iota(jnp.int32, (1, LANES), 1)
    row_id = lax.broadcasted_iota(jnp.int32, (ATT_TILE, ATT_TILE), 0)
    col_id = lax.broadcasted_iota(jnp.int32, (ATT_TILE, ATT_TILE), 1)
    causal = col_id < row_id

    def key_tile(j, qm, carry, diagonal):
        c0 = pl.multiple_of(j * ATT_TILE, ATT_TILE)
        kt = k_ref[pl.ds(c0, ATT_TILE), :]
        vt = v_ref[pl.ds(c0, ATT_TILE), :]
        z = _dot_nt(qm, kt)
        t = jnp.log(1.0 + jnp.exp(-jnp.abs(z)))
        log_sig = jnp.minimum(z, 0.0) - t
        log_fail = log_sig - z
        if diagonal:
            log_fail = jnp.where(causal, log_fail, 0.0)
        hi = log_fail.astype(BF16)
        lo = (log_fail - hi.astype(F32)).astype(BF16)
        later = _dot(hi, upper_ref[...]) + _dot(lo, upper_ref[...])
        w = jnp.exp(log_sig + later + carry)
        if diagonal:
            w = jnp.where(causal, w, 0.0)
        pv = _dot(w.astype(BF16), vt)
        return pv, carry + jnp.sum(log_fail, axis=-1, keepdims=True)

    def q_block(i, _):
        r0 = pl.multiple_of(i * ATT_TILE, ATT_TILE)
        q = q_ref[pl.ds(r0, ATT_TILE), :]
        for head in range(LANES // HEAD_DIM):
            in_head = (lane >= head * HEAD_DIM) & (lane < (head + 1) * HEAD_DIM)
            qm = jnp.where(in_head, q, jnp.zeros_like(q))
            pv, carry = key_tile(i, qm, jnp.zeros((ATT_TILE, 1), F32), True)
            acc_ref[head] = pv

            def more(state):
                j, _, cmax = state
                return jnp.logical_and(j >= 0, cmax > F32_EXP_ZERO)

            def sweep(state):
                j, carry, _ = state
                pv, carry = key_tile(j, qm, carry, False)
                acc_ref[head] += pv
                return j - 1, carry, jnp.max(carry)

            lax.while_loop(more, sweep, (i - 1, carry, jnp.max(carry)))
        o_ref[pl.ds(r0, ATT_TILE), :] = jnp.where(lane < HEAD_DIM, acc_ref[0], acc_ref[1])
        return 0

    lax.fori_loop(0, n_tiles, q_block, 0)


def _sb_attn(q, k, v, upper):
    b, s, _ = q.shape
    blk = pl.BlockSpec((None, s, LANES), lambda bi, hp: (bi, 0, hp))
    return pl.pallas_call(
        _sb_attn_kernel,
        grid=(b, SB_WIDTH // LANES),
        in_specs=[blk, blk, blk,
                  pl.BlockSpec((ATT_TILE, ATT_TILE), lambda bi, hp: (0, 0))],
        out_specs=blk,
        out_shape=jax.ShapeDtypeStruct((b, s, SB_WIDTH), F32),
        scratch_shapes=[pltpu.VMEM((LANES // HEAD_DIM, ATT_TILE, LANES), F32)],
        compiler_params=pltpu.CompilerParams(
            dimension_semantics=("arbitrary", "arbitrary"), vmem_limit_bytes=VMEM_LIMIT),
        name="sb_attn",
    )(q, k, v, upper)


def _mem_kv_kernel(mem_ref, g_ref, w_ref, kg_ref, ones_ref, mk_ref, mv_ref):
    x = mem_ref[...]
    ms = jnp.mean(x * x, axis=-1, keepdims=True)
    h = (x * lax.rsqrt(ms + EPS) * g_ref[...]).astype(BF16)
    kv = _dot(h, w_ref[...])
    mk = _head_rms(kv[:, :XA_WIDTH], ones_ref[...]) * kg_ref[...]
    mk_ref[...] = mk.astype(BF16)
    mv_ref[...] = kv[:, XA_WIDTH:].astype(BF16)


def _mem_kv(mem, mem_g, w_mem_kv, kg, ones):
    b = mem.shape[0]
    const = lambda bi: (0, 0)
    per_b = lambda bi: (bi, 0, 0)
    return pl.pallas_call(
        _mem_kv_kernel,
        grid=(b,),
        in_specs=[
            pl.BlockSpec((None, N_MEM, D_MODEL), per_b),
            pl.BlockSpec((1, D_MODEL), const),
            pl.BlockSpec((D_MODEL, 2 * XA_WIDTH), const),
            pl.BlockSpec((1, XA_WIDTH), const),
            pl.BlockSpec((MXU_WIDTH, MXU_WIDTH), const),
        ],
        out_specs=[pl.BlockSpec((None, N_MEM, XA_WIDTH), per_b),
                   pl.BlockSpec((None, N_MEM, XA_WIDTH), per_b)],
        out_shape=[jax.ShapeDtypeStruct((b, N_MEM, XA_WIDTH), BF16),
                   jax.ShapeDtypeStruct((b, N_MEM, XA_WIDTH), BF16)],
        compiler_params=pltpu.CompilerParams(
            dimension_semantics=("arbitrary",), vmem_limit_bytes=VMEM_LIMIT),
        name="mem_kv",
    )(mem, mem_g, w_mem_kv, kg, ones)


def _mix_out_kernel(x_ref, rest_ref, sb_ref, mk_ref, mv_ref, cw_ref, cb_ref, wg_ref,
                    bg_ref, lam_ref, xq_g_ref, ones_ref, wo_ref, o_ref, ext_ref, h_ref):
    tm = x_ref.shape[0]

    @pl.when(pl.program_id(1) == 0)
    def _():
        ext_ref[0:SUBLANES, :] = jnp.zeros((SUBLANES, LRU_WIDTH), F32)
        h_ref[...] = jnp.zeros_like(h_ref)

    sb_gate = rest_ref[:, 0:SB_WIDTH]
    lru_x = rest_ref[:, SB_WIDTH:SB_WIDTH + LRU_WIDTH]
    lru_gate = rest_ref[:, SB_WIDTH + LRU_WIDTH:SB_WIDTH + 2 * LRU_WIDTH]
    xa_q = rest_ref[:, SB_WIDTH + 2 * LRU_WIDTH:SB_WIDTH + 2 * LRU_WIDTH + XA_WIDTH]
    xa_gate = rest_ref[:, SB_WIDTH + 2 * LRU_WIDTH + XA_WIDTH:]

    ext_ref[SUBLANES:SUBLANES + tm, :] = lru_x
    xc = cb_ref[...] + cw_ref[CONV_WIDTH - 1:CONV_WIDTH, :] * lru_x
    for tap in range(CONV_WIDTH - 1):
        shift = CONV_WIDTH - 1 - tap
        xc = xc + cw_ref[tap:tap + 1, :] * ext_ref[SUBLANES - shift:SUBLANES - shift + tm, :]
    ext_ref[0:SUBLANES, :] = ext_ref[tm:tm + SUBLANES, :]

    gates = _dot(xc.astype(BF16), wg_ref[...]) + bg_ref[...]
    r = _sigmoid(gates[:, :LRU_WIDTH])
    i_gate = _sigmoid(gates[:, LRU_WIDTH:])
    neg_lam = -lam_ref[...]
    softplus_neg_lam = jnp.maximum(neg_lam, 0.0) + jnp.log(1.0 + jnp.exp(-jnp.abs(neg_lam)))
    log_a = (-LRU_C) * r * softplus_neg_lam
    a = jnp.exp(log_a)
    u = jnp.sqrt(1.0 - jnp.exp(2.0 * log_a)) * (i_gate * xc)
    row = lax.broadcasted_iota(jnp.int32, (tm, LRU_WIDTH), 0)
    d = 1
    while d < tm:
        keep = row >= d
        a_prev = jnp.where(keep, pltpu.roll(a, d, 0), 1.0)
        u_prev = jnp.where(keep, pltpu.roll(u, d, 0), 0.0)
        u = u + a * u_prev
        a = a * a_prev
        d *= 2
    h = u + a * h_ref[0:1, :]
    h_ref[...] = jnp.broadcast_to(h[tm - 1:tm, :], h_ref.shape)

    lane = lax.broadcasted_iota(jnp.int32, (1, XA_WIDTH), 1)
    qn = _head_rms(xa_q, ones_ref[...]) * xq_g_ref[...]
    mk = mk_ref[...]
    mv = mv_ref[...]
    xa = jnp.zeros((tm, XA_WIDTH), F32)
    for head in range(XA_WIDTH // HEAD_DIM):
        in_head = (lane >= head * HEAD_DIM) & (lane < (head + 1) * HEAD_DIM)
        qh = jnp.where(in_head, qn, 0.0).astype(BF16)
        s = _dot_nt(qh, mk)
        p = jnp.exp(s - jnp.max(s, axis=-1, keepdims=True))
        denom = jnp.sum(p, axis=-1, keepdims=True)
        oh = _dot(p.astype(BF16), mv)
        xa = jnp.where(in_head, oh / denom, xa)

    def silu(g):
        return g * _sigmoid(g)

    y = jnp.concatenate([sb_ref[...] * silu(sb_gate), h * silu(lru_gate),
                         xa * silu(xa_gate)], axis=1).astype(BF16)
    o_ref[...] = x_ref[...] + _dot(y, wo_ref[...])


def _mix_out(x, rest, sb, mk, mv, conv_w, conv_b, w_gates, b_gates, lam, xq_g, ones, w_out):
    b, s, _ = x.shape
    tile = lambda w: pl.BlockSpec((None, ROW_TILE, w), lambda bi, si: (bi, si, 0))
    per_b = lambda r, w: pl.BlockSpec((None, r, w), lambda bi, si: (bi, 0, 0))
    const = lambda r, w: pl.BlockSpec((r, w), lambda bi, si: (0, 0))
    return pl.pallas_call(
        _mix_out_kernel,
        grid=(b, s // ROW_TILE),
        in_specs=[
            tile(D_MODEL), tile(D_REST), tile(SB_WIDTH),
            per_b(N_MEM, XA_WIDTH), per_b(N_MEM, XA_WIDTH),
            const(CONV_WIDTH, LRU_WIDTH), const(1, LRU_WIDTH),
            const(LRU_WIDTH, 2 * LRU_WIDTH), const(1, 2 * LRU_WIDTH),
            const(1, LRU_WIDTH), const(1, XA_WIDTH),
            const(MXU_WIDTH, MXU_WIDTH), const(D_MODEL, D_MODEL),
        ],
        out_specs=tile(D_MODEL),
        out_shape=jax.ShapeDtypeStruct((b, s, D_MODEL), F32),
        scratch_shapes=[pltpu.VMEM((ROW_TILE + SUBLANES, LRU_WIDTH), F32),
                        pltpu.VMEM((SUBLANES, LRU_WIDTH), F32)],
        compiler_params=pltpu.CompilerParams(
            dimension_semantics=("arbitrary", "arbitrary"), vmem_limit_bytes=VMEM_LIMIT),
        name="mix_out",
    )(x, rest, sb, mk, mv, conv_w, conv_b, w_gates, b_gates, lam, xq_g, ones, w_out)


def _block_diag(w):
    n, d, _ = w.shape
    eye = jnp.eye(n, dtype=w.dtype)
    return (eye[:, None, :, None] * w[:, :, None, :]).reshape(n * d, n * d)


def kernel(x, mem, norm_g, w_in, sb_q_g, sb_k_g, conv_w, conv_b, w_rg, b_rg, w_ig, b_ig,
           lru_lambda, xa_q_g, xa_k_g, mem_g, w_mem_kv, w_out):
    b, s, d = x.shape
    depth = norm_g.shape[0]
    scale = HEAD_DIM ** -0.5
    idx = jnp.arange(MXU_WIDTH)
    group_ones = (idx[:, None] // HEAD_DIM == idx[None, :] // HEAD_DIM).astype(BF16)
    upper = (idx[:, None] > idx[None, :]).astype(BF16)

    for l in range(depth):
        qg = (jnp.tile(sb_q_g[l], SB_WIDTH // HEAD_DIM) * scale).reshape(1, SB_WIDTH)
        kg = jnp.tile(sb_k_g[l], SB_WIDTH // HEAD_DIM).reshape(1, SB_WIDTH)
        q, k, v, rest = _in_proj(x.reshape(b * s, d), norm_g[l].reshape(1, d),
                                 w_in[l].astype(BF16), qg, kg, group_ones)
        sb = _sb_attn(q.reshape(b, s, SB_WIDTH), k.reshape(b, s, SB_WIDTH),
                      v.reshape(b, s, SB_WIDTH), upper)
        mkg = jnp.tile(xa_k_g[l], XA_WIDTH // HEAD_DIM).reshape(1, XA_WIDTH)
        mk, mv = _mem_kv(mem, mem_g[l].reshape(1, d), w_mem_kv[l].astype(BF16), mkg,
                         group_ones)
        w_gates = jnp.concatenate([_block_diag(w_rg[l]), _block_diag(w_ig[l])],
                                  axis=1).astype(BF16)
        b_gates = jnp.concatenate([b_rg[l], b_ig[l]]).reshape(1, 2 * LRU_WIDTH)
        xq_g = (jnp.tile(xa_q_g[l], XA_WIDTH // HEAD_DIM) * scale).reshape(1, XA_WIDTH)
        x = _mix_out(x, rest.reshape(b, s, D_REST), sb, mk, mv, conv_w[l],
                     conv_b[l].reshape(1, LRU_WIDTH), w_gates, b_gates,
                     lru_lambda[l].reshape(1, LRU_WIDTH), xq_g, group_ones,
                     w_out[l].astype(BF16))
    return x
```

```python
import math

import jax
import jax.numpy as jnp
from jax import lax
from jax.experimental import pallas as pl
from jax.experimental.pallas import tpu as pltpu

D_MODEL = 1024
HEAD_DIM = 64
SB_WIDTH = 512
LRU_WIDTH = 256
XA_WIDTH = 256
N_MEM = 256
CONV_WIDTH = 4
LRU_C = 8.0
EPS = 1e-6
D_IN = 4 * SB_WIDTH + 2 * LRU_WIDTH + 2 * XA_WIDTH
D_REST = D_IN - 3 * SB_WIDTH

MXU_WIDTH = 256
LANES = 128
SUBLANES = 8
ROW_TILE = 512
ATT_TILE = MXU_WIDTH
ATT_WINDOW = 2 * ATT_TILE
ATT_UNROLL = 2
VMEM_LIMIT = 48 * 1024 * 1024

LOG2_E = math.log2(math.e)
F32_EXP2_ZERO = -150.0
MASKED_LOG2 = -1e30

F32 = jnp.float32
BF16 = jnp.bfloat16


def _dot(a, b):
    return jnp.dot(a, b, preferred_element_type=F32)


def _dot_nt(a, b):
    return lax.dot_general(a, b, (((1,), (1,)), ((), ())), preferred_element_type=F32)


def _head_rms(t, group_ones):
    outs = []
    for j in range(t.shape[1] // MXU_WIDTH):
        tj = t[:, j * MXU_WIDTH:(j + 1) * MXU_WIDTH]
        ss = _dot((tj * tj).astype(BF16), group_ones)
        outs.append(tj * lax.rsqrt(ss * (1.0 / HEAD_DIM) + EPS))
    return outs[0] if len(outs) == 1 else jnp.concatenate(outs, axis=1)


def _sigmoid(x):
    return 0.5 * jnp.tanh(0.5 * x) + 0.5


def _silu(x):
    half = 0.5 * x
    return half * jnp.tanh(half) + half


def _in_proj_kernel(x_ref, g_ref, w_ref, qg_ref, kg_ref, ones_ref,
                    q_ref, k_ref, v_ref, rest_ref):
    x = x_ref[...]
    ms = jnp.mean(x * x, axis=-1, keepdims=True)
    h = (x * lax.rsqrt(ms + EPS) * g_ref[...]).astype(BF16)
    ones = ones_ref[...]
    q = _dot(h, w_ref[:, 0:SB_WIDTH])
    q_ref[...] = (_head_rms(q, ones) * qg_ref[...]).astype(BF16)
    k = _dot(h, w_ref[:, SB_WIDTH:2 * SB_WIDTH])
    k_ref[...] = (_head_rms(k, ones) * kg_ref[...]).astype(BF16)
    v_ref[...] = _dot(h, w_ref[:, 2 * SB_WIDTH:3 * SB_WIDTH]).astype(BF16)
    for c in range(3 * SB_WIDTH, D_IN, SB_WIDTH):
        rest_ref[:, c - 3 * SB_WIDTH:c - 2 * SB_WIDTH] = _dot(h, w_ref[:, c:c + SB_WIDTH])


def _in_proj(x2d, norm_g, w_in, qg, kg, ones):
    m = x2d.shape[0]
    row = lambda i: (i, 0)
    const = lambda i: (0, 0)
    return pl.pallas_call(
        _in_proj_kernel,
        grid=(m // ROW_TILE,),
        in_specs=[
            pl.BlockSpec((ROW_TILE, D_MODEL), row),
            pl.BlockSpec((1, D_MODEL), const),
            pl.BlockSpec((D_MODEL, D_IN), const),
            pl.BlockSpec((1, SB_WIDTH), const),
            pl.BlockSpec((1, SB_WIDTH), const),
            pl.BlockSpec((MXU_WIDTH, MXU_WIDTH), const),
        ],
        out_specs=[
            pl.BlockSpec((ROW_TILE, SB_WIDTH), row),
            pl.BlockSpec((ROW_TILE, SB_WIDTH), row),
            pl.BlockSpec((ROW_TILE, SB_WIDTH), row),
            pl.BlockSpec((ROW_TILE, D_REST), row),
        ],
        out_shape=[
            jax.ShapeDtypeStruct((m, SB_WIDTH), BF16),
            jax.ShapeDtypeStruct((m, SB_WIDTH), BF16),
            jax.ShapeDtypeStruct((m, SB_WIDTH), BF16),
            jax.ShapeDtypeStruct((m, D_REST), F32),
        ],
        compiler_params=pltpu.CompilerParams(
            dimension_semantics=("arbitrary",), vmem_limit_bytes=VMEM_LIMIT),
        name="in_proj",
    )(x2d, norm_g, w_in, qg, kg, ones)


def _log2_terms(z):
    t = jnp.log2(1.0 + jnp.exp2(-jnp.abs(z)))
    log_sig = jnp.minimum(z, 0.0) - t
    return log_sig, log_sig - z


def _split_bf16(x):
    hi = x.astype(BF16)
    return hi, (x - hi.astype(F32)).astype(BF16)


def _sb_attn_kernel(q_ref, k_ref, v_ref, upper2_ref, o_ref, *scratch):
    seq = q_ref.shape[0]
    n_blocks = seq // ATT_TILE
    assert ATT_UNROLL % 2 == 0
    heads = range(LANES // HEAD_DIM)
    sets = (scratch[0:3], scratch[3:6])
    carry_ref, cmax_ref = scratch[6:8]
    lane = lax.broadcasted_iota(jnp.int32, (1, LANES), 1)
    row_id = lax.broadcasted_iota(jnp.int32, (ATT_TILE, ATT_TILE), 0)
    col_id = lax.broadcasted_iota(jnp.int32, (ATT_TILE, ATT_TILE), 1)
    causal = col_id < row_id

    def tile_start(j):
        if isinstance(j, int):
            return j * ATT_TILE
        return pl.multiple_of(j * ATT_TILE, ATT_TILE)

    def masked_queries(i):
        q = q_ref[pl.ds(tile_start(i), ATT_TILE), :]
        return [jnp.where((lane >= h * HEAD_DIM) & (lane < (h + 1) * HEAD_DIM), q,
                          jnp.zeros_like(q)) for h in heads]

    def window_start(i):
        return tile_start(max(i - 1, 0) if isinstance(i, int) else jnp.maximum(i - 1, 0))

    def block_rows(i):
        return pl.ds(tile_start(i), ATT_TILE)

    def scores(i, bufs, first=False):
        hl_ref, ls_ref, rs_ref = bufs
        kk = k_ref[pl.ds(window_start(i), ATT_WINDOW), :]
        z = _dot_nt(jnp.concatenate(masked_queries(i), axis=0), kk)
        cmax = None
        for h in heads:
            q_rows = slice(h * ATT_TILE, (h + 1) * ATT_TILE)
            row_sums = []
            for half in range(2):
                keys = slice(half * ATT_TILE, (half + 1) * ATT_TILE)
                if first and half == 1:
                    log_sig = jnp.full((ATT_TILE, ATT_TILE), MASKED_LOG2, F32)
                    log_fail = jnp.zeros((ATT_TILE, ATT_TILE), F32)
                else:
                    log_sig, log_fail = _log2_terms(z[q_rows, keys])
                    if first or half == 1:
                        log_sig = jnp.where(causal, log_sig, MASKED_LOG2)
                        log_fail = jnp.where(causal, log_fail, 0.0)
                hi, lo = _split_bf16(log_fail)
                hl_rows = slice((2 * h + half) * ATT_TILE, (2 * h + half + 1) * ATT_TILE)
                hl_ref[hl_rows, 0:ATT_TILE] = hi
                hl_ref[hl_rows, ATT_TILE:ATT_WINDOW] = lo
                ls_ref[q_rows, keys] = log_sig
                row_sums.append(jnp.sum(log_fail, axis=-1, keepdims=True))
            rs_ref[h] = row_sums[1]
            carry = row_sums[0] + row_sums[1]
            carry_ref[i, h] = carry
            m = jnp.max(carry)
            cmax = m if cmax is None else jnp.maximum(cmax, m)
        cmax_ref[i] = cmax

    def weights(i, bufs):
        hl_ref, ls_ref, rs_ref = bufs
        vv = v_ref[pl.ds(window_start(i), ATT_WINDOW), :]
        later = _dot(hl_ref[...], upper2_ref[...])
        ws = []
        for h in heads:
            q_rows = slice(h * ATT_TILE, (h + 1) * ATT_TILE)
            old = slice(2 * h * ATT_TILE, (2 * h + 1) * ATT_TILE)
            new = slice((2 * h + 1) * ATT_TILE, (2 * h + 2) * ATT_TILE)
            x_old = ls_ref[q_rows, 0:ATT_TILE] + (later[old] + rs_ref[h])
            x_new = ls_ref[q_rows, ATT_TILE:ATT_WINDOW] + later[new]
            ws.append(jnp.concatenate([jnp.exp2(x_old), jnp.exp2(x_new)], axis=1))
        pv = _dot(jnp.concatenate(ws, axis=0).astype(BF16), vv)
        o_ref[block_rows(i), :] = jnp.where(lane < HEAD_DIM, pv[:ATT_TILE], pv[ATT_TILE:])

    def finish(i, _):
        qms = masked_queries(i)

        def more(state):
            j, _, cmax = state
            return jnp.logical_and(j >= 0, cmax > F32_EXP2_ZERO)

        def sweep(state):
            j, carries, _ = state
            c0 = pl.multiple_of(j * ATT_TILE, ATT_TILE)
            kt = k_ref[pl.ds(c0, ATT_TILE), :]
            vt = v_ref[pl.ds(c0, ATT_TILE), :]
            new, pvs = [], []
            for h in heads:
                log_sig, log_fail = _log2_terms(_dot_nt(qms[h], kt))
                hi, lo = _split_bf16(log_fail)
                later = _dot(jnp.concatenate([hi, lo], axis=1), upper2_ref[...])
                w = jnp.exp2(log_sig + later + carries[h])
                pvs.append(_dot(w.astype(BF16), vt))
                new.append(carries[h] + jnp.sum(log_fail, axis=-1, keepdims=True))
            o_ref[block_rows(i), :] += jnp.where(lane < HEAD_DIM, pvs[0], pvs[1])
            return j - 1, tuple(new), jnp.max(jnp.maximum(new[0], new[1]))

        carries = tuple(carry_ref[i, h] for h in heads)
        lax.while_loop(more, sweep, (jnp.maximum(i - 1, 0) - 1, carries, cmax_ref[i]))
        return 0

    def step(i, bufs, other_bufs):
        scores(i, bufs)
        weights(i - 1, other_bufs)

    def unrolled_steps(p, _):
        for k in range(ATT_UNROLL):
            step(ATT_UNROLL * p + 1 + k, sets[(1 + k) % 2], sets[k % 2])
        return 0

    scores(0, sets[0], first=True)
    n_main = (n_blocks - 1) // ATT_UNROLL
    lax.fori_loop(0, n_main, unrolled_steps, 0)
    for i in range(ATT_UNROLL * n_main + 1, n_blocks):
        step(i, sets[i % 2], sets[(i - 1) % 2])
    weights(n_blocks - 1, sets[(n_blocks - 1) % 2])
    lax.fori_loop(0, n_blocks, finish, 0)


def _sb_attn(q, k, v, upper2):
    b, s, _ = q.shape
    n_heads = LANES // HEAD_DIM
    blk = pl.BlockSpec((None, s, LANES), lambda bi, hp: (bi, 0, hp))
    return pl.pallas_call(
        _sb_attn_kernel,
        grid=(b, SB_WIDTH // LANES),
        in_specs=[blk, blk, blk,
                  pl.BlockSpec((ATT_WINDOW, ATT_TILE), lambda bi, hp: (0, 0))],
        out_specs=blk,
        out_shape=jax.ShapeDtypeStruct((b, s, SB_WIDTH), F32),
        scratch_shapes=2 * [
            pltpu.VMEM((n_heads * ATT_WINDOW, ATT_WINDOW), BF16),
            pltpu.VMEM((n_heads * ATT_TILE, ATT_WINDOW), F32),
            pltpu.VMEM((n_heads, ATT_TILE, 1), F32),
        ] + [
            pltpu.VMEM((s // ATT_TILE, n_heads, ATT_TILE, 1), F32),
            pltpu.SMEM((s // ATT_TILE,), F32),
        ],
        compiler_params=pltpu.CompilerParams(
            dimension_semantics=("arbitrary", "arbitrary"), vmem_limit_bytes=VMEM_LIMIT),
        name="sb_attn",
    )(q, k, v, upper2)


def _mem_kv_kernel(mem_ref, g_ref, w_ref, kg_ref, ones_ref, mk_ref, mv_ref):
    x = mem_ref[...]
    ms = jnp.mean(x * x, axis=-1, keepdims=True)
    h = (x * lax.rsqrt(ms + EPS) * g_ref[...]).astype(BF16)
    kv = _dot(h, w_ref[...])
    mk = _head_rms(kv[:, :XA_WIDTH], ones_ref[...]) * kg_ref[...]
    mk_ref[...] = mk.astype(BF16)
    mv_ref[...] = kv[:, XA_WIDTH:].astype(BF16)


def _mem_kv(mem, mem_g, w_mem_kv, kg, ones):
    b = mem.shape[0]
    const = lambda bi: (0, 0)
    per_b = lambda bi: (bi, 0, 0)
    return pl.pallas_call(
        _mem_kv_kernel,
        grid=(b,),
        in_specs=[
            pl.BlockSpec((None, N_MEM, D_MODEL), per_b),
            pl.BlockSpec((1, D_MODEL), const),
            pl.BlockSpec((D_MODEL, 2 * XA_WIDTH), const),
            pl.BlockSpec((1, XA_WIDTH), const),
            pl.BlockSpec((MXU_WIDTH, MXU_WIDTH), const),
        ],
        out_specs=[pl.BlockSpec((None, N_MEM, XA_WIDTH), per_b),
                   pl.BlockSpec((None, N_MEM, XA_WIDTH), per_b)],
        out_shape=[jax.ShapeDtypeStruct((b, N_MEM, XA_WIDTH), BF16),
                   jax.ShapeDtypeStruct((b, N_MEM, XA_WIDTH), BF16)],
        compiler_params=pltpu.CompilerParams(
            dimension_semantics=("arbitrary",), vmem_limit_bytes=VMEM_LIMIT),
        name="mem_kv",
    )(mem, mem_g, w_mem_kv, kg, ones)


def _mix_out_kernel(x_ref, rest_ref, sb_ref, mk_ref, mv_ref, cw_ref, cb_ref, wg_ref,
                    bg_ref, lam_ref, xq_g_ref, ones_ref, wo_ref, o_ref, ext_ref, h_ref):
    tm = x_ref.shape[0]

    @pl.when(pl.program_id(1) == 0)
    def _():
        ext_ref[0:SUBLANES, :] = jnp.zeros((SUBLANES, LRU_WIDTH), F32)
        h_ref[...] = jnp.zeros_like(h_ref)

    sb_gate = rest_ref[:, 0:SB_WIDTH]
    o_ref[...] = x_ref[...] + _dot((sb_ref[...] * _silu(sb_gate)).astype(BF16),
                                   wo_ref[0:SB_WIDTH, :])
    lru_x = rest_ref[:, SB_WIDTH:SB_WIDTH + LRU_WIDTH]
    lru_gate = rest_ref[:, SB_WIDTH + LRU_WIDTH:SB_WIDTH + 2 * LRU_WIDTH]
    xa_q = rest_ref[:, SB_WIDTH + 2 * LRU_WIDTH:SB_WIDTH + 2 * LRU_WIDTH + XA_WIDTH]
    xa_gate = rest_ref[:, SB_WIDTH + 2 * LRU_WIDTH + XA_WIDTH:]

    ext_ref[SUBLANES:SUBLANES + tm, :] = lru_x
    xc = cb_ref[...] + cw_ref[CONV_WIDTH - 1:CONV_WIDTH, :] * lru_x
    for tap in range(CONV_WIDTH - 1):
        shift = CONV_WIDTH - 1 - tap
        xc = xc + cw_ref[tap:tap + 1, :] * ext_ref[SUBLANES - shift:SUBLANES - shift + tm, :]
    ext_ref[0:SUBLANES, :] = ext_ref[tm:tm + SUBLANES, :]

    gates = _dot(xc.astype(BF16), wg_ref[...]) + bg_ref[...]
    r = _sigmoid(gates[:, :LRU_WIDTH])
    i_gate = _sigmoid(gates[:, LRU_WIDTH:])
    neg_lam = -lam_ref[...]
    softplus_neg_lam = jnp.maximum(neg_lam, 0.0) + jnp.log(1.0 + jnp.exp(-jnp.abs(neg_lam)))
    log_a = (-LRU_C) * r * softplus_neg_lam
    a = jnp.exp(log_a)
    u = jnp.sqrt(1.0 - jnp.exp(2.0 * log_a)) * (i_gate * xc)
    n_groups = tm // SUBLANES
    a = a.reshape(n_groups, SUBLANES, LRU_WIDTH)
    u = u.reshape(n_groups, SUBLANES, LRU_WIDTH)
    sub = lax.broadcasted_iota(jnp.int32, (1, SUBLANES, LRU_WIDTH), 1)
    d = 1
    while d < SUBLANES:
        keep = sub >= d
        a_prev = jnp.where(keep, pltpu.roll(a, d, 1), 1.0)
        u_prev = jnp.where(keep, pltpu.roll(u, d, 1), 0.0)
        u = u + a * u_prev
        a = a * a_prev
        d *= 2
    h_in = h_ref[...]
    h_before = []
    for g in range(n_groups):
        h_before.append(h_in)
        a_tot = jnp.broadcast_to(a[g, SUBLANES - 1:SUBLANES, :], (SUBLANES, LRU_WIDTH))
        u_tot = jnp.broadcast_to(u[g, SUBLANES - 1:SUBLANES, :], (SUBLANES, LRU_WIDTH))
        h_in = a_tot * h_in + u_tot
    h_ref[...] = h_in
    h = (u + a * jnp.stack(h_before, axis=0)).reshape(tm, LRU_WIDTH)

    lane = lax.broadcasted_iota(jnp.int32, (1, XA_WIDTH), 1)
    qn = _head_rms(xa_q, ones_ref[...]) * xq_g_ref[...]
    mk = mk_ref[...]
    mv = mv_ref[...]
    xa = jnp.zeros((tm, XA_WIDTH), F32)
    for head in range(XA_WIDTH // HEAD_DIM):
        in_head = (lane >= head * HEAD_DIM) & (lane < (head + 1) * HEAD_DIM)
        qh = jnp.where(in_head, qn, 0.0).astype(BF16)
        s = _dot_nt(qh, mk)
        p = jnp.exp2(s - jnp.max(s, axis=-1, keepdims=True))
        denom = jnp.sum(p, axis=-1, keepdims=True)
        oh = _dot(p.astype(BF16), mv)
        xa = jnp.where(in_head, oh / denom, xa)

    y = jnp.concatenate([h * _silu(lru_gate), xa * _silu(xa_gate)], axis=1).astype(BF16)
    o_ref[...] += _dot(y, wo_ref[SB_WIDTH:, :])


def _mix_out(x, rest, sb, mk, mv, conv_w, conv_b, w_gates, b_gates, lam, xq_g, ones, w_out):
    b, s, _ = x.shape
    tile = lambda w: pl.BlockSpec((None, ROW_TILE, w), lambda bi, si: (bi, si, 0))
    per_b = lambda r, w: pl.BlockSpec((None, r, w), lambda bi, si: (bi, 0, 0))
    const = lambda r, w: pl.BlockSpec((r, w), lambda bi, si: (0, 0))
    return pl.pallas_call(
        _mix_out_kernel,
        grid=(b, s // ROW_TILE),
        in_specs=[
            tile(D_MODEL), tile(D_REST), tile(SB_WIDTH),
            per_b(N_MEM, XA_WIDTH), per_b(N_MEM, XA_WIDTH),
            const(CONV_WIDTH, LRU_WIDTH), const(1, LRU_WIDTH),
            const(LRU_WIDTH, 2 * LRU_WIDTH), const(1, 2 * LRU_WIDTH),
            const(1, LRU_WIDTH), const(1, XA_WIDTH),
            const(MXU_WIDTH, MXU_WIDTH), const(D_MODEL, D_MODEL),
        ],
        out_specs=tile(D_MODEL),
        out_shape=jax.ShapeDtypeStruct((b, s, D_MODEL), F32),
        scratch_shapes=[pltpu.VMEM((ROW_TILE + SUBLANES, LRU_WIDTH), F32),
                        pltpu.VMEM((SUBLANES, LRU_WIDTH), F32)],
        compiler_params=pltpu.CompilerParams(
            dimension_semantics=("arbitrary", "arbitrary"), vmem_limit_bytes=VMEM_LIMIT),
        name="mix_out",
    )(x, rest, sb, mk, mv, conv_w, conv_b, w_gates, b_gates, lam, xq_g, ones, w_out)


def _block_diag(w):
    n, d, _ = w.shape
    eye = jnp.eye(n, dtype=w.dtype)
    return (eye[:, None, :, None] * w[:, :, None, :]).reshape(n * d, n * d)


def kernel(x, mem, norm_g, w_in, sb_q_g, sb_k_g, conv_w, conv_b, w_rg, b_rg, w_ig, b_ig,
           lru_lambda, xa_q_g, xa_k_g, mem_g, w_mem_kv, w_out):
    b, s, d = x.shape
    depth = norm_g.shape[0]
    scale = HEAD_DIM ** -0.5
    idx = jnp.arange(MXU_WIDTH)
    group_ones = (idx[:, None] // HEAD_DIM == idx[None, :] // HEAD_DIM).astype(BF16)
    upper = (idx[:, None] > idx[None, :]).astype(BF16)
    upper2 = jnp.concatenate([upper, upper], axis=0)

    for l in range(depth):
        qg = (jnp.tile(sb_q_g[l], SB_WIDTH // HEAD_DIM) * (scale * LOG2_E)).reshape(1, SB_WIDTH)
        kg = jnp.tile(sb_k_g[l], SB_WIDTH // HEAD_DIM).reshape(1, SB_WIDTH)
        q, k, v, rest = _in_proj(x.reshape(b * s, d), norm_g[l].reshape(1, d),
                                 w_in[l].astype(BF16), qg, kg, group_ones)
        sb = _sb_attn(q.reshape(b, s, SB_WIDTH), k.reshape(b, s, SB_WIDTH),
                      v.reshape(b, s, SB_WIDTH), upper2)
        mkg = jnp.tile(xa_k_g[l], XA_WIDTH // HEAD_DIM).reshape(1, XA_WIDTH)
        mk, mv = _mem_kv(mem, mem_g[l].reshape(1, d), w_mem_kv[l].astype(BF16), mkg,
                         group_ones)
        w_gates = jnp.concatenate([_block_diag(w_rg[l]), _block_diag(w_ig[l])],
                                  axis=1).astype(BF16)
        b_gates = jnp.concatenate([b_rg[l], b_ig[l]]).reshape(1, 2 * LRU_WIDTH)
        xq_g = (jnp.tile(xa_q_g[l], XA_WIDTH // HEAD_DIM) * (scale * LOG2_E)).reshape(1, XA_WIDTH)
        x = _mix_out(x, rest.reshape(b, s, D_REST), sb, mk, mv, conv_w[l],
                     conv_b[l].reshape(1, LRU_WIDTH), w_gates, b_gates,
                     lru_lambda[l].reshape(1, LRU_WIDTH), xq_g, group_ones,
                     w_out[l].astype(BF16))
    return x
```

```python
import math

import jax
import jax.numpy as jnp
from jax import lax
from jax.experimental import pallas as pl
from jax.experimental.pallas import tpu as pltpu

D_MODEL = 1024
HEAD_DIM = 64
SB_WIDTH = 512
LRU_WIDTH = 256
XA_WIDTH = 256
N_MEM = 256
CONV_WIDTH = 4
LRU_C = 8.0
EPS = 1e-6
D_IN = 4 * SB_WIDTH + 2 * LRU_WIDTH + 2 * XA_WIDTH
D_REST = D_IN - 3 * SB_WIDTH

MXU_WIDTH = 256
LANES = 128
SUBLANES = 8
ROW_TILE = 512
ATT_TILE = MXU_WIDTH
ATT_QUAD = ATT_TILE // 2
ATT_WINDOW = 2 * ATT_TILE
ATT_UNROLL = 2
VMEM_LIMIT = 48 * 1024 * 1024

LOG2_E = math.log2(math.e)
F32_EXP2_ZERO = -150.0
MASKED_LOG2 = -1e30

F32 = jnp.float32
BF16 = jnp.bfloat16


def _dot(a, b):
    return jnp.dot(a, b, preferred_element_type=F32)


def _dot_nt(a, b):
    return lax.dot_general(a, b, (((1,), (1,)), ((), ())), preferred_element_type=F32)


def _head_rms(t, group_ones):
    outs = []
    for j in range(t.shape[1] // MXU_WIDTH):
        tj = t[:, j * MXU_WIDTH:(j + 1) * MXU_WIDTH]
        ss = _dot((tj * tj).astype(BF16), group_ones)
        outs.append(tj * lax.rsqrt(ss * (1.0 / HEAD_DIM) + EPS))
    return outs[0] if len(outs) == 1 else jnp.concatenate(outs, axis=1)


def _sigmoid(x):
    return 0.5 * jnp.tanh(0.5 * x) + 0.5


def _silu(x):
    half = 0.5 * x
    return half * jnp.tanh(half) + half


def _in_proj_kernel(x_ref, g_ref, w_ref, qg_ref, kg_ref, ones_ref,
                    q_ref, k_ref, v_ref, rest_ref):
    x = x_ref[...]
    ms = jnp.mean(x * x, axis=-1, keepdims=True)
    h = (x * lax.rsqrt(ms + EPS) * g_ref[...]).astype(BF16)
    ones = ones_ref[...]
    q = _dot(h, w_ref[:, 0:SB_WIDTH])
    q_ref[...] = (_head_rms(q, ones) * qg_ref[...]).astype(BF16)
    k = _dot(h, w_ref[:, SB_WIDTH:2 * SB_WIDTH])
    k_ref[...] = (_head_rms(k, ones) * kg_ref[...]).astype(BF16)
    v_ref[...] = _dot(h, w_ref[:, 2 * SB_WIDTH:3 * SB_WIDTH]).astype(BF16)
    for c in range(3 * SB_WIDTH, D_IN, SB_WIDTH):
        rest_ref[:, c - 3 * SB_WIDTH:c - 2 * SB_WIDTH] = _dot(h, w_ref[:, c:c + SB_WIDTH])


def _in_proj(x2d, norm_g, w_in, qg, kg, ones):
    m = x2d.shape[0]
    row = lambda i: (i, 0)
    const = lambda i: (0, 0)
    return pl.pallas_call(
        _in_proj_kernel,
        grid=(m // ROW_TILE,),
        in_specs=[
            pl.BlockSpec((ROW_TILE, D_MODEL), row),
            pl.BlockSpec((1, D_MODEL), const),
            pl.BlockSpec((D_MODEL, D_IN), const),
            pl.BlockSpec((1, SB_WIDTH), const),
            pl.BlockSpec((1, SB_WIDTH), const),
            pl.BlockSpec((MXU_WIDTH, MXU_WIDTH), const),
        ],
        out_specs=[
            pl.BlockSpec((ROW_TILE, SB_WIDTH), row),
            pl.BlockSpec((ROW_TILE, SB_WIDTH), row),
            pl.BlockSpec((ROW_TILE, SB_WIDTH), row),
            pl.BlockSpec((ROW_TILE, D_REST), row),
        ],
        out_shape=[
            jax.ShapeDtypeStruct((m, SB_WIDTH), BF16),
            jax.ShapeDtypeStruct((m, SB_WIDTH), BF16),
            jax.ShapeDtypeStruct((m, SB_WIDTH), BF16),
            jax.ShapeDtypeStruct((m, D_REST), F32),
        ],
        compiler_params=pltpu.CompilerParams(
            dimension_semantics=("arbitrary",), vmem_limit_bytes=VMEM_LIMIT),
        name="in_proj",
    )(x2d, norm_g, w_in, qg, kg, ones)


def _log2_terms(z):
    t = jnp.log2(1.0 + jnp.exp2(-jnp.abs(z)))
    log_sig = jnp.minimum(z, 0.0) - t
    return log_sig, log_sig - z


def _sb_attn_kernel(q_ref, k_ref, v_ref, upper_ref, o_ref, *scratch):
    seq = q_ref.shape[0]
    n_blocks = seq // ATT_TILE
    assert ATT_UNROLL % 2 == 0
    heads = range(LANES // HEAD_DIM)
    sets = (scratch[0:3], scratch[3:6])
    carry_ref, cmax_ref = scratch[6:8]
    lane = lax.broadcasted_iota(jnp.int32, (1, LANES), 1)
    causal = (lax.broadcasted_iota(jnp.int32, (ATT_QUAD, ATT_QUAD), 1)
              < lax.broadcasted_iota(jnp.int32, (ATT_QUAD, ATT_QUAD), 0))

    def tile_start(j):
        if isinstance(j, int):
            return j * ATT_TILE
        return pl.multiple_of(j * ATT_TILE, ATT_TILE)

    def masked_queries(i):
        q = q_ref[pl.ds(tile_start(i), ATT_TILE), :]
        return [jnp.where((lane >= h * HEAD_DIM) & (lane < (h + 1) * HEAD_DIM), q,
                          jnp.zeros_like(q)) for h in heads]

    def window_start(i):
        return tile_start(max(i - 1, 0) if isinstance(i, int) else jnp.maximum(i - 1, 0))

    def block_rows(i):
        return pl.ds(tile_start(i), ATT_TILE)

    def put(bufs, h, tile, r0, c0, log_sig, log_fail):
        lf_ref, ls_ref, _ = bufs
        nr, nc = log_sig.shape
        lf_row = (2 * h + tile) * ATT_TILE + r0
        lf_ref[lf_row:lf_row + nr, c0:c0 + nc] = log_fail.astype(BF16)
        ls_ref[h * ATT_TILE + r0:h * ATT_TILE + r0 + nr,
               tile * ATT_TILE + c0:tile * ATT_TILE + c0 + nc] = log_sig
        return jnp.sum(log_fail, axis=-1, keepdims=True)

    def put_masked(bufs, h, tile, r0, c0, nr, nc):
        put(bufs, h, tile, r0, c0, jnp.full((nr, nc), MASKED_LOG2, F32), jnp.zeros((nr, nc), F32))

    def full_tile(bufs, h, tile, z):
        return put(bufs, h, tile, 0, 0, *_log2_terms(z))

    def diagonal_tile(bufs, h, tile, z):
        def masked(log_sig, log_fail):
            return jnp.where(causal, log_sig, MASKED_LOG2), jnp.where(causal, log_fail, 0.0)

        top = put(bufs, h, tile, 0, 0, *masked(*_log2_terms(z[:ATT_QUAD, :ATT_QUAD])))
        bottom = put(bufs, h, tile, ATT_QUAD, 0, *_log2_terms(z[ATT_QUAD:, :ATT_QUAD]))
        bottom = bottom + put(bufs, h, tile, ATT_QUAD, ATT_QUAD,
                              *masked(*_log2_terms(z[ATT_QUAD:, ATT_QUAD:])))
        return jnp.concatenate([top, bottom], axis=0)

    def scores(i, bufs, first=False):
        rs_ref = bufs[2]
        kk = k_ref[pl.ds(window_start(i), ATT_WINDOW), :]
        z = _dot_nt(jnp.concatenate(masked_queries(i), axis=0), kk)
        cmax = None
        for h in heads:
            zh = z[h * ATT_TILE:(h + 1) * ATT_TILE]
            if first:
                put_masked(bufs, h, 0, 0, ATT_QUAD, ATT_QUAD, ATT_QUAD)
                rs_old = diagonal_tile(bufs, h, 0, zh[:, :ATT_TILE])
                put_masked(bufs, h, 1, 0, 0, ATT_TILE, ATT_TILE)
                rs_new = jnp.zeros((ATT_TILE, 1), F32)
            else:
                rs_old = full_tile(bufs, h, 0, zh[:, :ATT_TILE])
                rs_new = diagonal_tile(bufs, h, 1, zh[:, ATT_TILE:])
            rs_ref[h] = rs_new
            carry = rs_old + rs_new
            carry_ref[i, h] = carry
            m = jnp.max(carry)
            cmax = m if cmax is None else jnp.maximum(cmax, m)
        cmax_ref[i] = cmax

    def weights(i, bufs):
        lf_ref, ls_ref, rs_ref = bufs
        vv = v_ref[pl.ds(window_start(i), ATT_WINDOW), :]
        later = _dot(lf_ref[...], upper_ref[...])
        ws = []
        for h in heads:
            q_rows = slice(h * ATT_TILE, (h + 1) * ATT_TILE)
            old = slice(2 * h * ATT_TILE, (2 * h + 1) * ATT_TILE)
            new = slice((2 * h + 1) * ATT_TILE, (2 * h + 2) * ATT_TILE)
            x_old = ls_ref[q_rows, 0:ATT_TILE] + (later[old] + rs_ref[h])
            x_new = ls_ref[q_rows, ATT_TILE:ATT_WINDOW] + later[new]
            ws.append(jnp.concatenate([jnp.exp2(x_old), jnp.exp2(x_new)], axis=1))
        pv = _dot(jnp.concatenate(ws, axis=0).astype(BF16), vv)
        o_ref[block_rows(i), :] = jnp.where(lane < HEAD_DIM, pv[:ATT_TILE], pv[ATT_TILE:])

    def finish(i, _):
        qms = masked_queries(i)

        def more(state):
            j, _, cmax = state
            return jnp.logical_and(j >= 0, cmax > F32_EXP2_ZERO)

        def sweep(state):
            j, carries, _ = state
            c0 = pl.multiple_of(j * ATT_TILE, ATT_TILE)
            kt = k_ref[pl.ds(c0, ATT_TILE), :]
            vt = v_ref[pl.ds(c0, ATT_TILE), :]
            new, pvs = [], []
            for h in heads:
                log_sig, log_fail = _log2_terms(_dot_nt(qms[h], kt))
                later = _dot(log_fail.astype(BF16), upper_ref[...])
                w = jnp.exp2(log_sig + later + carries[h])
                pvs.append(_dot(w.astype(BF16), vt))
                new.append(carries[h] + jnp.sum(log_fail, axis=-1, keepdims=True))
            o_ref[block_rows(i), :] += jnp.where(lane < HEAD_DIM, pvs[0], pvs[1])
            return j - 1, tuple(new), jnp.max(jnp.maximum(new[0], new[1]))

        carries = tuple(carry_ref[i, h] for h in heads)
        lax.while_loop(more, sweep, (jnp.maximum(i - 1, 0) - 1, carries, cmax_ref[i]))
        return 0

    def step(i, bufs, other_bufs):
        scores(i, bufs)
        weights(i - 1, other_bufs)

    def unrolled_steps(p, _):
        for k in range(ATT_UNROLL):
            step(ATT_UNROLL * p + 1 + k, sets[(1 + k) % 2], sets[k % 2])
        return 0

    for h in heads:
        put_masked(sets[1], h, 1, 0, ATT_QUAD, ATT_QUAD, ATT_QUAD)
    scores(0, sets[0], first=True)
    n_main = (n_blocks - 1) // ATT_UNROLL
    lax.fori_loop(0, n_main, unrolled_steps, 0)
    for i in range(ATT_UNROLL * n_main + 1, n_blocks):
        step(i, sets[i % 2], sets[(i - 1) % 2])
    weights(n_blocks - 1, sets[(n_blocks - 1) % 2])
    lax.fori_loop(0, n_blocks, finish, 0)


def _sb_attn(q, k, v, upper):
    b, s, _ = q.shape
    n_heads = LANES // HEAD_DIM
    blk = pl.BlockSpec((None, s, LANES), lambda bi, hp: (bi, 0, hp))
    return pl.pallas_call(
        _sb_attn_kernel,
        grid=(b, SB_WIDTH // LANES),
        in_specs=[blk, blk, blk,
                  pl.BlockSpec((ATT_TILE, ATT_TILE), lambda bi, hp: (0, 0))],
        out_specs=blk,
        out_shape=jax.ShapeDtypeStruct((b, s, SB_WIDTH), F32),
        scratch_shapes=2 * [
            pltpu.VMEM((n_heads * ATT_WINDOW, ATT_TILE), BF16),
            pltpu.VMEM((n_heads * ATT_TILE, ATT_WINDOW), F32),
            pltpu.VMEM((n_heads, ATT_TILE, 1), F32),
        ] + [
            pltpu.VMEM((s // ATT_TILE, n_heads, ATT_TILE, 1), F32),
            pltpu.SMEM((s // ATT_TILE,), F32),
        ],
        compiler_params=pltpu.CompilerParams(
            dimension_semantics=("arbitrary", "arbitrary"), vmem_limit_bytes=VMEM_LIMIT),
        name="sb_attn",
    )(q, k, v, upper)


def _mem_kv_kernel(mem_ref, g_ref, w_ref, kg_ref, ones_ref, mk_ref, mv_ref):
    x = mem_ref[...]
    ms = jnp.mean(x * x, axis=-1, keepdims=True)
    h = (x * lax.rsqrt(ms + EPS) * g_ref[...]).astype(BF16)
    kv = _dot(h, w_ref[...])
    mk = _head_rms(kv[:, :XA_WIDTH], ones_ref[...]) * kg_ref[...]
    mk_ref[...] = mk.astype(BF16)
    mv_ref[...] = kv[:, XA_WIDTH:].astype(BF16)


def _mem_kv(mem, mem_g, w_mem_kv, kg, ones):
    b = mem.shape[0]
    const = lambda bi: (0, 0)
    per_b = lambda bi: (bi, 0, 0)
    return pl.pallas_call(
        _mem_kv_kernel,
        grid=(b,),
        in_specs=[
            pl.BlockSpec((None, N_MEM, D_MODEL), per_b),
            pl.BlockSpec((1, D_MODEL), const),
            pl.BlockSpec((D_MODEL, 2 * XA_WIDTH), const),
            pl.BlockSpec((1, XA_WIDTH), const),
            pl.BlockSpec((MXU_WIDTH, MXU_WIDTH), const),
        ],
        out_specs=[pl.BlockSpec((None, N_MEM, XA_WIDTH), per_b),
                   pl.BlockSpec((None, N_MEM, XA_WIDTH), per_b)],
        out_shape=[jax.ShapeDtypeStruct((b, N_MEM, XA_WIDTH), BF16),
                   jax.ShapeDtypeStruct((b, N_MEM, XA_WIDTH), BF16)],
        compiler_params=pltpu.CompilerParams(
            dimension_semantics=("arbitrary",), vmem_limit_bytes=VMEM_LIMIT),
        name="mem_kv",
    )(mem, mem_g, w_mem_kv, kg, ones)


def _mix_out_kernel(x_ref, rest_ref, sb_ref, mk_ref, mv_ref, cw_ref, cb_ref, wg_ref,
                    bg_ref, lam_ref, xq_g_ref, ones_ref, wo_ref, o_ref, ext_ref, h_ref):
    tm = x_ref.shape[0]

    @pl.when(pl.program_id(1) == 0)
    def _():
        ext_ref[0:SUBLANES, :] = jnp.zeros((SUBLANES, LRU_WIDTH), F32)
        h_ref[...] = jnp.zeros_like(h_ref)

    sb_gate = rest_ref[:, 0:SB_WIDTH]
    o_ref[...] = x_ref[...] + _dot((sb_ref[...] * _silu(sb_gate)).astype(BF16),
                                   wo_ref[0:SB_WIDTH, :])
    lru_x = rest_ref[:, SB_WIDTH:SB_WIDTH + LRU_WIDTH]
    lru_gate = rest_ref[:, SB_WIDTH + LRU_WIDTH:SB_WIDTH + 2 * LRU_WIDTH]
    xa_q = rest_ref[:, SB_WIDTH + 2 * LRU_WIDTH:SB_WIDTH + 2 * LRU_WIDTH + XA_WIDTH]
    xa_gate = rest_ref[:, SB_WIDTH + 2 * LRU_WIDTH + XA_WIDTH:]

    ext_ref[SUBLANES:SUBLANES + tm, :] = lru_x
    xc = cb_ref[...] + cw_ref[CONV_WIDTH - 1:CONV_WIDTH, :] * lru_x
    for tap in range(CONV_WIDTH - 1):
        shift = CONV_WIDTH - 1 - tap
        xc = xc + cw_ref[tap:tap + 1, :] * ext_ref[SUBLANES - shift:SUBLANES - shift + tm, :]
    ext_ref[0:SUBLANES, :] = ext_ref[tm:tm + SUBLANES, :]

    gates = _dot(xc.astype(BF16), wg_ref[...]) + bg_ref[...]
    r = _sigmoid(gates[:, :LRU_WIDTH])
    i_gate = _sigmoid(gates[:, LRU_WIDTH:])
    neg_lam = -lam_ref[...]
    softplus_neg_lam = jnp.maximum(neg_lam, 0.0) + jnp.log(1.0 + jnp.exp(-jnp.abs(neg_lam)))
    log_a = (-LRU_C) * r * softplus_neg_lam
    a = jnp.exp(log_a)
    u = jnp.sqrt(1.0 - jnp.exp(2.0 * log_a)) * (i_gate * xc)
    n_groups = tm // SUBLANES
    a = a.reshape(n_groups, SUBLANES, LRU_WIDTH)
    u = u.reshape(n_groups, SUBLANES, LRU_WIDTH)
    sub = lax.broadcasted_iota(jnp.int32, (1, SUBLANES, LRU_WIDTH), 1)
    d = 1
    while d < SUBLANES:
        keep = sub >= d
        a_prev = jnp.where(keep, pltpu.roll(a, d, 1), 1.0)
        u_prev = jnp.where(keep, pltpu.roll(u, d, 1), 0.0)
        u = u + a * u_prev
        a = a * a_prev
        d *= 2
    h_in = h_ref[...]
    h_before = []
    for g in range(n_groups):
        h_before.append(h_in)
        a_tot = jnp.broadcast_to(a[g, SUBLANES - 1:SUBLANES, :], (SUBLANES, LRU_WIDTH))
        u_tot = jnp.broadcast_to(u[g, SUBLANES - 1:SUBLANES, :], (SUBLANES, LRU_WIDTH))
        h_in = a_tot * h_in + u_tot
    h_ref[...] = h_in
    h = (u + a * jnp.stack(h_before, axis=0)).reshape(tm, LRU_WIDTH)

    lane = lax.broadcasted_iota(jnp.int32, (1, XA_WIDTH), 1)
    qn = _head_rms(xa_q, ones_ref[...]) * xq_g_ref[...]
    mk = mk_ref[...]
    mv = mv_ref[...]
    xa = jnp.zeros((tm, XA_WIDTH), F32)
    for head in range(XA_WIDTH // HEAD_DIM):
        in_head = (lane >= head * HEAD_DIM) & (lane < (head + 1) * HEAD_DIM)
        qh = jnp.where(in_head, qn, 0.0).astype(BF16)
        s = _dot_nt(qh, mk)
        p = jnp.exp2(s - jnp.max(s, axis=-1, keepdims=True))
        denom = jnp.sum(p, axis=-1, keepdims=True)
        oh = _dot(p.astype(BF16), mv)
        xa = jnp.where(in_head, oh / denom, xa)

    y = jnp.concatenate([h * _silu(lru_gate), xa * _silu(xa_gate)], axis=1).astype(BF16)
    o_ref[...] += _dot(y, wo_ref[SB_WIDTH:, :])


def _mix_out(x, rest, sb, mk, mv, conv_w, conv_b, w_gates, b_gates, lam, xq_g, ones, w_out):
    b, s, _ = x.shape
    tile = lambda w: pl.BlockSpec((None, ROW_TILE, w), lambda bi, si: (bi, si, 0))
    per_b = lambda r, w: pl.BlockSpec((None, r, w), lambda bi, si: (bi, 0, 0))
    const = lambda r, w: pl.BlockSpec((r, w), lambda bi, si: (0, 0))
    return pl.pallas_call(
        _mix_out_kernel,
        grid=(b, s // ROW_TILE),
        in_specs=[
            tile(D_MODEL), tile(D_REST), tile(SB_WIDTH),
            per_b(N_MEM, XA_WIDTH), per_b(N_MEM, XA_WIDTH),
            const(CONV_WIDTH, LRU_WIDTH), const(1, LRU_WIDTH),
            const(LRU_WIDTH, 2 * LRU_WIDTH), const(1, 2 * LRU_WIDTH),
            const(1, LRU_WIDTH), const(1, XA_WIDTH),
            const(MXU_WIDTH, MXU_WIDTH), const(D_MODEL, D_MODEL),
        ],
        out_specs=tile(D_MODEL),
        out_shape=jax.ShapeDtypeStruct((b, s, D_MODEL), F32),
        scratch_shapes=[pltpu.VMEM((ROW_TILE + SUBLANES, LRU_WIDTH), F32),
                        pltpu.VMEM((SUBLANES, LRU_WIDTH), F32)],
        compiler_params=pltpu.CompilerParams(
            dimension_semantics=("arbitrary", "arbitrary"), vmem_limit_bytes=VMEM_LIMIT),
        name="mix_out",
    )(x, rest, sb, mk, mv, conv_w, conv_b, w_gates, b_gates, lam, xq_g, ones, w_out)


def _block_diag(w):
    n, d, _ = w.shape
    eye = jnp.eye(n, dtype=w.dtype)
    return (eye[:, None, :, None] * w[:, :, None, :]).reshape(n * d, n * d)


def kernel(x, mem, norm_g, w_in, sb_q_g, sb_k_g, conv_w, conv_b, w_rg, b_rg, w_ig, b_ig,
           lru_lambda, xa_q_g, xa_k_g, mem_g, w_mem_kv, w_out):
    b, s, d = x.shape
    depth = norm_g.shape[0]
    scale = HEAD_DIM ** -0.5
    idx = jnp.arange(MXU_WIDTH)
    group_ones = (idx[:, None] // HEAD_DIM == idx[None, :] // HEAD_DIM).astype(BF16)
    upper = (idx[:, None] > idx[None, :]).astype(BF16)

    for l in range(depth):
        qg = (jnp.tile(sb_q_g[l], SB_WIDTH // HEAD_DIM) * (scale * LOG2_E)).reshape(1, SB_WIDTH)
        kg = jnp.tile(sb_k_g[l], SB_WIDTH // HEAD_DIM).reshape(1, SB_WIDTH)
        q, k, v, rest = _in_proj(x.reshape(b * s, d), norm_g[l].reshape(1, d),
                                 w_in[l].astype(BF16), qg, kg, group_ones)
        sb = _sb_attn(q.reshape(b, s, SB_WIDTH), k.reshape(b, s, SB_WIDTH),
                      v.reshape(b, s, SB_WIDTH), upper)
        mkg = jnp.tile(xa_k_g[l], XA_WIDTH // HEAD_DIM).reshape(1, XA_WIDTH)
        mk, mv = _mem_kv(mem, mem_g[l].reshape(1, d), w_mem_kv[l].astype(BF16), mkg,
                         group_ones)
        w_gates = jnp.concatenate([_block_diag(w_rg[l]), _block_diag(w_ig[l])],
                                  axis=1).astype(BF16)
        b_gates = jnp.concatenate([b_rg[l], b_ig[l]]).reshape(1, 2 * LRU_WIDTH)
        xq_g = (jnp.tile(xa_q_g[l], XA_WIDTH // HEAD_DIM) * (scale * LOG2_E)).reshape(1, XA_WIDTH)
        x = _mix_out(x, rest.reshape(b, s, D_REST), sb, mk, mv, conv_w[l],
                     conv_b[l].reshape(1, LRU_WIDTH), w_gates, b_gates,
                     lru_lambda[l].reshape(1, LRU_WIDTH), xq_g, group_ones,
                     w_out[l].astype(BF16))
    return x
```

```python
import math

import jax
import jax.numpy as jnp
from jax import lax
from jax.experimental import pallas as pl
from jax.experimental.pallas import tpu as pltpu

D_MODEL = 1024
HEAD_DIM = 64
SB_WIDTH = 512
LRU_WIDTH = 256
XA_WIDTH = 256
N_MEM = 256
CONV_WIDTH = 4
LRU_C = 8.0
EPS = 1e-6
D_IN = 4 * SB_WIDTH + 2 * LRU_WIDTH + 2 * XA_WIDTH
D_MIX = SB_WIDTH + LRU_WIDTH + XA_WIDTH

MXU_WIDTH = 256
LANES = 128
SUBLANES = 8
ROW_TILE = 1024
ATT_TILE = MXU_WIDTH
ATT_QUAD = ATT_TILE // 2
ATT_WINDOW = 2 * ATT_TILE
ATT_UNROLL = 2
VMEM_LIMIT = 48 * 1024 * 1024

LOG2_E = math.log2(math.e)
F32_EXP2_ZERO = -150.0
MASKED_LOG2 = -1e30

F32 = jnp.float32
BF16 = jnp.bfloat16


def _dot(a, b):
    return jnp.dot(a, b, preferred_element_type=F32)


def _dot_nt(a, b):
    return lax.dot_general(a, b, (((1,), (1,)), ((), ())), preferred_element_type=F32)


def _head_rms(t, group_ones):
    outs = []
    for j in range(t.shape[1] // MXU_WIDTH):
        tj = t[:, j * MXU_WIDTH:(j + 1) * MXU_WIDTH]
        ss = _dot((tj * tj).astype(BF16), group_ones)
        outs.append(tj * lax.rsqrt(ss * (1.0 / HEAD_DIM) + EPS))
    return outs[0] if len(outs) == 1 else jnp.concatenate(outs, axis=1)


def _sigmoid(x):
    return 0.5 * jnp.tanh(0.5 * x) + 0.5


def _silu(x):
    half = 0.5 * x
    return half * jnp.tanh(half) + half


def _in_proj_kernel(x_ref, g_ref, w_ref, qg_ref, kg_ref, ones_ref,
                    q_ref, k_ref, v_ref, gates_ref, feats_ref):
    x = x_ref[...]
    ms = jnp.mean(x * x, axis=-1, keepdims=True)
    h = (x * lax.rsqrt(ms + EPS) * g_ref[...]).astype(BF16)
    ones = ones_ref[...]

    def proj(c0, width):
        return _dot(h, w_ref[:, c0:c0 + width])

    q_ref[...] = (_head_rms(proj(0, SB_WIDTH), ones) * qg_ref[...]).astype(BF16)
    k_ref[...] = (_head_rms(proj(SB_WIDTH, SB_WIDTH), ones) * kg_ref[...]).astype(BF16)
    v_ref[...] = proj(2 * SB_WIDTH, SB_WIDTH).astype(BF16)
    gates_ref[:, 0:SB_WIDTH] = _silu(proj(3 * SB_WIDTH, SB_WIDTH)).astype(BF16)
    c0 = 4 * SB_WIDTH
    for g, width in enumerate((LRU_WIDTH, XA_WIDTH)):
        pair = proj(c0, 2 * width)
        feats_ref[:, g * LRU_WIDTH:g * LRU_WIDTH + width] = pair[:, :width]
        gates_ref[:, SB_WIDTH + g * LRU_WIDTH:SB_WIDTH + g * LRU_WIDTH + width] = (
            _silu(pair[:, width:]).astype(BF16))
        c0 += 2 * width


def _in_proj(x2d, norm_g, w_in, qg, kg, ones):
    m = x2d.shape[0]
    row = lambda i: (i, 0)
    const = lambda i: (0, 0)
    return pl.pallas_call(
        _in_proj_kernel,
        grid=(m // ROW_TILE,),
        in_specs=[
            pl.BlockSpec((ROW_TILE, D_MODEL), row),
            pl.BlockSpec((1, D_MODEL), const),
            pl.BlockSpec((D_MODEL, D_IN), const),
            pl.BlockSpec((1, SB_WIDTH), const),
            pl.BlockSpec((1, SB_WIDTH), const),
            pl.BlockSpec((MXU_WIDTH, MXU_WIDTH), const),
        ],
        out_specs=[
            pl.BlockSpec((ROW_TILE, SB_WIDTH), row),
            pl.BlockSpec((ROW_TILE, SB_WIDTH), row),
            pl.BlockSpec((ROW_TILE, SB_WIDTH), row),
            pl.BlockSpec((ROW_TILE, D_MIX), row),
            pl.BlockSpec((ROW_TILE, LRU_WIDTH + XA_WIDTH), row),
        ],
        out_shape=[
            jax.ShapeDtypeStruct((m, SB_WIDTH), BF16),
            jax.ShapeDtypeStruct((m, SB_WIDTH), BF16),
            jax.ShapeDtypeStruct((m, SB_WIDTH), BF16),
            jax.ShapeDtypeStruct((m, D_MIX), BF16),
            jax.ShapeDtypeStruct((m, LRU_WIDTH + XA_WIDTH), F32),
        ],
        compiler_params=pltpu.CompilerParams(
            dimension_semantics=("arbitrary",), vmem_limit_bytes=VMEM_LIMIT),
        name="in_proj",
    )(x2d, norm_g, w_in, qg, kg, ones)


def _log2_terms(z):
    t = jnp.log2(1.0 + jnp.exp2(-jnp.abs(z)))
    log_sig = jnp.minimum(z, 0.0) - t
    return log_sig, log_sig - z


def _sb_attn_kernel(q_ref, k_ref, v_ref, upper_ref, o_ref, *scratch):
    seq = q_ref.shape[0]
    n_blocks = seq // ATT_TILE
    assert ATT_UNROLL % 2 == 0
    heads = range(LANES // HEAD_DIM)
    sets = (scratch[0:3], scratch[3:6])
    carry_ref, cmax_ref = scratch[6:8]
    lane = lax.broadcasted_iota(jnp.int32, (1, LANES), 1)
    causal = (lax.broadcasted_iota(jnp.int32, (ATT_QUAD, ATT_QUAD), 1)
              < lax.broadcasted_iota(jnp.int32, (ATT_QUAD, ATT_QUAD), 0))

    def tile_start(j):
        if isinstance(j, int):
            return j * ATT_TILE
        return pl.multiple_of(j * ATT_TILE, ATT_TILE)

    def masked_queries(i):
        q = q_ref[pl.ds(tile_start(i), ATT_TILE), :]
        return [jnp.where((lane >= h * HEAD_DIM) & (lane < (h + 1) * HEAD_DIM), q,
                          jnp.zeros_like(q)) for h in heads]

    def window_start(i):
        return tile_start(max(i - 1, 0) if isinstance(i, int) else jnp.maximum(i - 1, 0))

    def block_rows(i):
        return pl.ds(tile_start(i), ATT_TILE)

    def put(bufs, h, tile, r0, c0, log_sig, log_fail):
        lf_ref, ls_ref, _ = bufs
        nr, nc = log_sig.shape
        lf_row = (2 * h + tile) * ATT_TILE + r0
        lf_ref[lf_row:lf_row + nr, c0:c0 + nc] = log_fail.astype(BF16)
        ls_ref[h * ATT_TILE + r0:h * ATT_TILE + r0 + nr,
               tile * ATT_TILE + c0:tile * ATT_TILE + c0 + nc] = log_sig
        return jnp.sum(log_fail, axis=-1, keepdims=True)

    def put_masked(bufs, h, tile, r0, c0, nr, nc):
        put(bufs, h, tile, r0, c0, jnp.full((nr, nc), MASKED_LOG2, F32), jnp.zeros((nr, nc), F32))

    def full_tile(bufs, h, tile, z):
        return put(bufs, h, tile, 0, 0, *_log2_terms(z))

    def diagonal_tile(bufs, h, tile, z):
        def masked(log_sig, log_fail):
            return jnp.where(causal, log_sig, MASKED_LOG2), jnp.where(causal, log_fail, 0.0)

        top = put(bufs, h, tile, 0, 0, *masked(*_log2_terms(z[:ATT_QUAD, :ATT_QUAD])))
        bottom = put(bufs, h, tile, ATT_QUAD, 0, *_log2_terms(z[ATT_QUAD:, :ATT_QUAD]))
        bottom = bottom + put(bufs, h, tile, ATT_QUAD, ATT_QUAD,
                              *masked(*_log2_terms(z[ATT_QUAD:, ATT_QUAD:])))
        return jnp.concatenate([top, bottom], axis=0)

    def scores(i, bufs, first=False):
        rs_ref = bufs[2]
        kk = k_ref[pl.ds(window_start(i), ATT_WINDOW), :]
        z = _dot_nt(jnp.concatenate(masked_queries(i), axis=0), kk)
        cmax = None
        for h in heads:
            zh = z[h * ATT_TILE:(h + 1) * ATT_TILE]
            if first:
                put_masked(bufs, h, 0, 0, ATT_QUAD, ATT_QUAD, ATT_QUAD)
                rs_old = diagonal_tile(bufs, h, 0, zh[:, :ATT_TILE])
                put_masked(bufs, h, 1, 0, 0, ATT_TILE, ATT_TILE)
                rs_new = jnp.zeros((ATT_TILE, 1), F32)
            else:
                rs_old = full_tile(bufs, h, 0, zh[:, :ATT_TILE])
                rs_new = diagonal_tile(bufs, h, 1, zh[:, ATT_TILE:])
            rs_ref[h] = rs_new
            carry = rs_old + rs_new
            carry_ref[i, h] = carry
            m = jnp.max(carry)
            cmax = m if cmax is None else jnp.maximum(cmax, m)
        cmax_ref[i] = cmax

    def weights(i, bufs):
        lf_ref, ls_ref, rs_ref = bufs
        vv = v_ref[pl.ds(window_start(i), ATT_WINDOW), :]
        later = _dot(lf_ref[...], upper_ref[...])
        ws = []
        for h in heads:
            q_rows = slice(h * ATT_TILE, (h + 1) * ATT_TILE)
            old = slice(2 * h * ATT_TILE, (2 * h + 1) * ATT_TILE)
            new = slice((2 * h + 1) * ATT_TILE, (2 * h + 2) * ATT_TILE)
            x_old = ls_ref[q_rows, 0:ATT_TILE] + (later[old] + rs_ref[h])
            x_new = ls_ref[q_rows, ATT_TILE:ATT_WINDOW] + later[new]
            ws.append(jnp.concatenate([jnp.exp2(x_old), jnp.exp2(x_new)], axis=1))
        pv = _dot(jnp.concatenate(ws, axis=0).astype(BF16), vv)
        o_ref[block_rows(i), :] = jnp.where(lane < HEAD_DIM, pv[:ATT_TILE], pv[ATT_TILE:])

    def finish(i, _):
        qms = masked_queries(i)

        def more(state):
            j, _, cmax = state
            return jnp.logical_and(j >= 0, cmax > F32_EXP2_ZERO)

        def sweep(state):
            j, carries, _ = state
            c0 = pl.multiple_of(j * ATT_TILE, ATT_TILE)
            kt = k_ref[pl.ds(c0, ATT_TILE), :]
            vt = v_ref[pl.ds(c0, ATT_TILE), :]
            new, pvs = [], []
            for h in heads:
                log_sig, log_fail = _log2_terms(_dot_nt(qms[h], kt))
                later = _dot(log_fail.astype(BF16), upper_ref[...])
                w = jnp.exp2(log_sig + later + carries[h])
                pvs.append(_dot(w.astype(BF16), vt))
                new.append(carries[h] + jnp.sum(log_fail, axis=-1, keepdims=True))
            o_ref[block_rows(i), :] += jnp.where(lane < HEAD_DIM, pvs[0], pvs[1])
            return j - 1, tuple(new), jnp.max(jnp.maximum(new[0], new[1]))

        carries = tuple(carry_ref[i, h] for h in heads)
        lax.while_loop(more, sweep, (jnp.maximum(i - 1, 0) - 1, carries, cmax_ref[i]))
        return 0

    def step(i, bufs, other_bufs):
        scores(i, bufs)
        weights(i - 1, other_bufs)

    def unrolled_steps(p, _):
        for k in range(ATT_UNROLL):
            step(ATT_UNROLL * p + 1 + k, sets[(1 + k) % 2], sets[k % 2])
        return 0

    for h in heads:
        put_masked(sets[1], h, 1, 0, ATT_QUAD, ATT_QUAD, ATT_QUAD)
    scores(0, sets[0], first=True)
    n_main = (n_blocks - 1) // ATT_UNROLL
    lax.fori_loop(0, n_main, unrolled_steps, 0)
    for i in range(ATT_UNROLL * n_main + 1, n_blocks):
        step(i, sets[i % 2], sets[(i - 1) % 2])
    weights(n_blocks - 1, sets[(n_blocks - 1) % 2])
    lax.fori_loop(0, n_blocks, finish, 0)


def _sb_attn(q, k, v, upper):
    b, s, _ = q.shape
    n_heads = LANES // HEAD_DIM
    blk = pl.BlockSpec((None, s, LANES), lambda bi, hp: (bi, 0, hp))
    return pl.pallas_call(
        _sb_attn_kernel,
        grid=(b, SB_WIDTH // LANES),
        in_specs=[blk, blk, blk,
                  pl.BlockSpec((ATT_TILE, ATT_TILE), lambda bi, hp: (0, 0))],
        out_specs=blk,
        out_shape=jax.ShapeDtypeStruct((b, s, SB_WIDTH), F32),
        scratch_shapes=2 * [
            pltpu.VMEM((n_heads * ATT_WINDOW, ATT_TILE), BF16),
            pltpu.VMEM((n_heads * ATT_TILE, ATT_WINDOW), F32),
            pltpu.VMEM((n_heads, ATT_TILE, 1), F32),
        ] + [
            pltpu.VMEM((s // ATT_TILE, n_heads, ATT_TILE, 1), F32),
            pltpu.SMEM((s // ATT_TILE,), F32),
        ],
        compiler_params=pltpu.CompilerParams(
            dimension_semantics=("arbitrary", "arbitrary"), vmem_limit_bytes=VMEM_LIMIT),
        name="sb_attn",
    )(q, k, v, upper)


def _mem_kv_kernel(mem_ref, g_ref, w_ref, kg_ref, ones_ref, mk_ref, mv_ref):
    x = mem_ref[...]
    ms = jnp.mean(x * x, axis=-1, keepdims=True)
    h = (x * lax.rsqrt(ms + EPS) * g_ref[...]).astype(BF16)
    kv = _dot(h, w_ref[...])
    mk = _head_rms(kv[:, :XA_WIDTH], ones_ref[...]) * kg_ref[...]
    mk_ref[...] = mk.astype(BF16)
    mv_ref[...] = kv[:, XA_WIDTH:].astype(BF16)


def _mem_kv(mem, mem_g, w_mem_kv, kg, ones):
    b = mem.shape[0]
    const = lambda bi: (0, 0)
    per_b = lambda bi: (bi, 0, 0)
    return pl.pallas_call(
        _mem_kv_kernel,
        grid=(b,),
        in_specs=[
            pl.BlockSpec((None, N_MEM, D_MODEL), per_b),
            pl.BlockSpec((1, D_MODEL), const),
            pl.BlockSpec((D_MODEL, 2 * XA_WIDTH), const),
            pl.BlockSpec((1, XA_WIDTH), const),
            pl.BlockSpec((MXU_WIDTH, MXU_WIDTH), const),
        ],
        out_specs=[pl.BlockSpec((None, N_MEM, XA_WIDTH), per_b),
                   pl.BlockSpec((None, N_MEM, XA_WIDTH), per_b)],
        out_shape=[jax.ShapeDtypeStruct((b, N_MEM, XA_WIDTH), BF16),
                   jax.ShapeDtypeStruct((b, N_MEM, XA_WIDTH), BF16)],
        compiler_params=pltpu.CompilerParams(
            dimension_semantics=("arbitrary",), vmem_limit_bytes=VMEM_LIMIT),
        name="mem_kv",
    )(mem, mem_g, w_mem_kv, kg, ones)


def _mix_out_kernel(x_ref, gates_ref, feats_ref, sb_ref, mk_ref, mv_ref, cw_ref, cb_ref,
                    wg_ref, bg_ref, lam_ref, xq_g_ref, ones_ref, wo_ref, o_ref, ext_ref,
                    h_ref):
    tm = x_ref.shape[0]

    @pl.when(pl.program_id(1) == 0)
    def _():
        ext_ref[0:SUBLANES, :] = jnp.zeros((SUBLANES, LRU_WIDTH), F32)
        h_ref[...] = jnp.zeros_like(h_ref)

    o_ref[...] = x_ref[...] + _dot(sb_ref[...].astype(BF16) * gates_ref[:, 0:SB_WIDTH],
                                   wo_ref[0:SB_WIDTH, :])
    lru_x = feats_ref[:, 0:LRU_WIDTH]
    xa_q = feats_ref[:, LRU_WIDTH:]

    ext_ref[SUBLANES:SUBLANES + tm, :] = lru_x
    xc = cb_ref[...] + cw_ref[CONV_WIDTH - 1:CONV_WIDTH, :] * lru_x
    for tap in range(CONV_WIDTH - 1):
        shift = CONV_WIDTH - 1 - tap
        xc = xc + cw_ref[tap:tap + 1, :] * ext_ref[SUBLANES - shift:SUBLANES - shift + tm, :]
    ext_ref[0:SUBLANES, :] = ext_ref[tm:tm + SUBLANES, :]

    gates = _dot(xc.astype(BF16), wg_ref[...]) + bg_ref[...]
    r = _sigmoid(gates[:, :LRU_WIDTH])
    i_gate = _sigmoid(gates[:, LRU_WIDTH:])
    neg_lam = -lam_ref[...]
    softplus_neg_lam = jnp.maximum(neg_lam, 0.0) + jnp.log(1.0 + jnp.exp(-jnp.abs(neg_lam)))
    log_a = (-LRU_C) * r * softplus_neg_lam
    a = jnp.exp(log_a)
    u = jnp.sqrt(1.0 - jnp.exp(2.0 * log_a)) * (i_gate * xc)
    n_groups = tm // SUBLANES
    a = a.reshape(n_groups, SUBLANES, LRU_WIDTH)
    u = u.reshape(n_groups, SUBLANES, LRU_WIDTH)
    sub = lax.broadcasted_iota(jnp.int32, (1, SUBLANES, LRU_WIDTH), 1)
    d = 1
    while d < SUBLANES:
        keep = sub >= d
        a_prev = jnp.where(keep, pltpu.roll(a, d, 1), 1.0)
        u_prev = jnp.where(keep, pltpu.roll(u, d, 1), 0.0)
        u = u + a * u_prev
        a = a * a_prev
        d *= 2
    h_in = h_ref[...]
    h_before = []
    for g in range(n_groups):
        h_before.append(h_in)
        a_tot = jnp.broadcast_to(a[g, SUBLANES - 1:SUBLANES, :], (SUBLANES, LRU_WIDTH))
        u_tot = jnp.broadcast_to(u[g, SUBLANES - 1:SUBLANES, :], (SUBLANES, LRU_WIDTH))
        h_in = a_tot * h_in + u_tot
    h_ref[...] = h_in
    h = (u + a * jnp.stack(h_before, axis=0)).reshape(tm, LRU_WIDTH)

    lane = lax.broadcasted_iota(jnp.int32, (1, XA_WIDTH), 1)
    qn = _head_rms(xa_q, ones_ref[...]) * xq_g_ref[...]
    mk = mk_ref[...]
    mv = mv_ref[...]
    xa = jnp.zeros((tm, XA_WIDTH), F32)
    for head in range(XA_WIDTH // HEAD_DIM):
        in_head = (lane >= head * HEAD_DIM) & (lane < (head + 1) * HEAD_DIM)
        qh = jnp.where(in_head, qn, 0.0).astype(BF16)
        s = _dot_nt(qh, mk)
        p = jnp.exp2(s - jnp.max(s, axis=-1, keepdims=True))
        denom = jnp.sum(p, axis=-1, keepdims=True)
        oh = _dot(p.astype(BF16), mv)
        xa = jnp.where(in_head, oh / denom, xa)

    y = jnp.concatenate([h, xa], axis=1).astype(BF16) * gates_ref[:, SB_WIDTH:]
    o_ref[...] += _dot(y, wo_ref[SB_WIDTH:, :])


def _mix_out(x, gates, feats, sb, mk, mv, conv_w, conv_b, w_gates, b_gates, lam, xq_g, ones,
             w_out):
    b, s, _ = x.shape
    tile = lambda w: pl.BlockSpec((None, ROW_TILE, w), lambda bi, si: (bi, si, 0))
    per_b = lambda r, w: pl.BlockSpec((None, r, w), lambda bi, si: (bi, 0, 0))
    const = lambda r, w: pl.BlockSpec((r, w), lambda bi, si: (0, 0))
    return pl.pallas_call(
        _mix_out_kernel,
        grid=(b, s // ROW_TILE),
        in_specs=[
            tile(D_MODEL), tile(D_MIX), tile(LRU_WIDTH + XA_WIDTH), tile(SB_WIDTH),
            per_b(N_MEM, XA_WIDTH), per_b(N_MEM, XA_WIDTH),
            const(CONV_WIDTH, LRU_WIDTH), const(1, LRU_WIDTH),
            const(LRU_WIDTH, 2 * LRU_WIDTH), const(1, 2 * LRU_WIDTH),
            const(1, LRU_WIDTH), const(1, XA_WIDTH),
            const(MXU_WIDTH, MXU_WIDTH), const(D_MODEL, D_MODEL),
        ],
        out_specs=tile(D_MODEL),
        out_shape=jax.ShapeDtypeStruct((b, s, D_MODEL), F32),
        scratch_shapes=[pltpu.VMEM((ROW_TILE + SUBLANES, LRU_WIDTH), F32),
                        pltpu.VMEM((SUBLANES, LRU_WIDTH), F32)],
        compiler_params=pltpu.CompilerParams(
            dimension_semantics=("arbitrary", "arbitrary"), vmem_limit_bytes=VMEM_LIMIT),
        name="mix_out",
    )(x, gates, feats, sb, mk, mv, conv_w, conv_b, w_gates, b_gates, lam, xq_g, ones, w_out)


def _block_diag(w):
    n, d, _ = w.shape
    eye = jnp.eye(n, dtype=w.dtype)
    return (eye[:, None, :, None] * w[:, :, None, :]).reshape(n * d, n * d)


def kernel(x, mem, norm_g, w_in, sb_q_g, sb_k_g, conv_w, conv_b, w_rg, b_rg, w_ig, b_ig,
           lru_lambda, xa_q_g, xa_k_g, mem_g, w_mem_kv, w_out):
    b, s, d = x.shape
    depth = norm_g.shape[0]
    scale = HEAD_DIM ** -0.5
    idx = jnp.arange(MXU_WIDTH)
    group_ones = (idx[:, None] // HEAD_DIM == idx[None, :] // HEAD_DIM).astype(BF16)
    upper = (idx[:, None] > idx[None, :]).astype(BF16)
    w_in, w_mem_kv, w_out = (w.astype(BF16) for w in (w_in, w_mem_kv, w_out))

    for l in range(depth):
        qg = (jnp.tile(sb_q_g[l], SB_WIDTH // HEAD_DIM) * (scale * LOG2_E)).reshape(1, SB_WIDTH)
        kg = jnp.tile(sb_k_g[l], SB_WIDTH // HEAD_DIM).reshape(1, SB_WIDTH)
        q, k, v, gates, feats = _in_proj(x.reshape(b * s, d), norm_g[l].reshape(1, d),
                                         w_in[l], qg, kg, group_ones)
        sb = _sb_attn(q.reshape(b, s, SB_WIDTH), k.reshape(b, s, SB_WIDTH),
                      v.reshape(b, s, SB_WIDTH), upper)
        mkg = jnp.tile(xa_k_g[l], XA_WIDTH // HEAD_DIM).reshape(1, XA_WIDTH)
        mk, mv = _mem_kv(mem, mem_g[l].reshape(1, d), w_mem_kv[l], mkg, group_ones)
        w_gates = jnp.concatenate([_block_diag(w_rg[l]), _block_diag(w_ig[l])],
                                  axis=1).astype(BF16)
        b_gates = jnp.concatenate([b_rg[l], b_ig[l]]).reshape(1, 2 * LRU_WIDTH)
        xq_g = (jnp.tile(xa_q_g[l], XA_WIDTH // HEAD_DIM) * (scale * LOG2_E)).reshape(1, XA_WIDTH)
        x = _mix_out(x, gates.reshape(b, s, D_MIX), feats.reshape(b, s, LRU_WIDTH + XA_WIDTH),
                     sb, mk, mv, conv_w[l], conv_b[l].reshape(1, LRU_WIDTH), w_gates, b_gates,
                     lru_lambda[l].reshape(1, LRU_WIDTH), xq_g, group_ones, w_out[l])
    return x
```

```python
import math

import jax
import jax.numpy as jnp
from jax import lax
from jax.experimental import pallas as pl
from jax.experimental.pallas import tpu as pltpu

D_MODEL = 1024
HEAD_DIM = 64
SB_WIDTH = 512
LRU_WIDTH = 256
XA_WIDTH = 256
N_MEM = 256
CONV_WIDTH = 4
LRU_C = 8.0
EPS = 1e-6
D_IN = 4 * SB_WIDTH + 2 * LRU_WIDTH + 2 * XA_WIDTH
D_MIX = SB_WIDTH + LRU_WIDTH + XA_WIDTH

MXU_WIDTH = 256
LANES = 128
SUBLANES = 8
ROW_TILE = 1024
ATT_TILE = MXU_WIDTH
ATT_QUAD = ATT_TILE // 2
ATT_WINDOW = 2 * ATT_TILE
ATT_UNROLL = 4
VMEM_LIMIT = 48 * 1024 * 1024

LOG2_E = math.log2(math.e)
F32_EXP2_ZERO = -150.0
MASKED_LOG2 = -1e30

F32 = jnp.float32
BF16 = jnp.bfloat16


def _dot(a, b):
    return jnp.dot(a, b, preferred_element_type=F32)


def _dot_nt(a, b):
    return lax.dot_general(a, b, (((1,), (1,)), ((), ())), preferred_element_type=F32)


def _head_rms(t, group_ones):
    outs = []
    for j in range(t.shape[1] // MXU_WIDTH):
        tj = t[:, j * MXU_WIDTH:(j + 1) * MXU_WIDTH]
        ss = _dot((tj * tj).astype(BF16), group_ones)
        outs.append(tj * lax.rsqrt(ss * (1.0 / HEAD_DIM) + EPS))
    return outs[0] if len(outs) == 1 else jnp.concatenate(outs, axis=1)


def _sigmoid(x):
    return 0.5 * jnp.tanh(0.5 * x) + 0.5


def _silu(x):
    half = 0.5 * x
    return half * jnp.tanh(half) + half


def _in_proj_kernel(x_ref, g_ref, w_ref, qg_ref, kg_ref, ones_ref,
                    q_ref, k_ref, v_ref, gates_ref, feats_ref):
    x = x_ref[...]
    ms = jnp.mean(x * x, axis=-1, keepdims=True)
    h = (x * lax.rsqrt(ms + EPS) * g_ref[...]).astype(BF16)
    ones = ones_ref[...]

    def proj(c0, width):
        return _dot(h, w_ref[:, c0:c0 + width])

    q_ref[...] = (_head_rms(proj(0, SB_WIDTH), ones) * qg_ref[...]).astype(BF16)
    k_ref[...] = (_head_rms(proj(SB_WIDTH, SB_WIDTH), ones) * kg_ref[...]).astype(BF16)
    v_ref[...] = proj(2 * SB_WIDTH, SB_WIDTH).astype(BF16)
    gates_ref[:, 0:SB_WIDTH] = _silu(proj(3 * SB_WIDTH, SB_WIDTH)).astype(BF16)
    c0 = 4 * SB_WIDTH
    for g, width in enumerate((LRU_WIDTH, XA_WIDTH)):
        pair = proj(c0, 2 * width)
        feats_ref[:, g * LRU_WIDTH:g * LRU_WIDTH + width] = pair[:, :width]
        gates_ref[:, SB_WIDTH + g * LRU_WIDTH:SB_WIDTH + g * LRU_WIDTH + width] = (
            _silu(pair[:, width:]).astype(BF16))
        c0 += 2 * width


def _in_proj(x2d, norm_g, w_in, layer, qg, kg, ones):
    m = x2d.shape[0]
    row = lambda i: (i, 0)
    const = lambda i: (0, 0)
    return pl.pallas_call(
        _in_proj_kernel,
        grid=(m // ROW_TILE,),
        in_specs=[
            pl.BlockSpec((ROW_TILE, D_MODEL), row),
            pl.BlockSpec((1, D_MODEL), const),
            pl.BlockSpec((None, D_MODEL, D_IN), lambda i: (layer, 0, 0)),
            pl.BlockSpec((1, SB_WIDTH), const),
            pl.BlockSpec((1, SB_WIDTH), const),
            pl.BlockSpec((MXU_WIDTH, MXU_WIDTH), const),
        ],
        out_specs=[
            pl.BlockSpec((ROW_TILE, SB_WIDTH), row),
            pl.BlockSpec((ROW_TILE, SB_WIDTH), row),
            pl.BlockSpec((ROW_TILE, SB_WIDTH), row),
            pl.BlockSpec((ROW_TILE, D_MIX), row),
            pl.BlockSpec((ROW_TILE, LRU_WIDTH + XA_WIDTH), row),
        ],
        out_shape=[
            jax.ShapeDtypeStruct((m, SB_WIDTH), BF16),
            jax.ShapeDtypeStruct((m, SB_WIDTH), BF16),
            jax.ShapeDtypeStruct((m, SB_WIDTH), BF16),
            jax.ShapeDtypeStruct((m, D_MIX), BF16),
            jax.ShapeDtypeStruct((m, LRU_WIDTH + XA_WIDTH), F32),
        ],
        compiler_params=pltpu.CompilerParams(
            dimension_semantics=("arbitrary",), vmem_limit_bytes=VMEM_LIMIT),
        name="in_proj",
    )(x2d, norm_g, w_in, qg, kg, ones)


def _log2_terms(z):
    t = jnp.log2(1.0 + jnp.exp2(-jnp.abs(z)))
    log_sig = jnp.minimum(z, 0.0) - t
    return log_sig, log_sig - z


def _sb_attn_kernel(q_ref, k_ref, v_ref, upper_ref, o_ref, *scratch):
    seq = q_ref.shape[0]
    n_blocks = seq // ATT_TILE
    assert ATT_UNROLL % 2 == 0
    heads = range(LANES // HEAD_DIM)
    sets = (scratch[0:3], scratch[3:6])
    carry_ref, cmax_ref = scratch[6:8]
    lane = lax.broadcasted_iota(jnp.int32, (1, LANES), 1)
    causal = (lax.broadcasted_iota(jnp.int32, (ATT_QUAD, ATT_QUAD), 1)
              < lax.broadcasted_iota(jnp.int32, (ATT_QUAD, ATT_QUAD), 0))

    def tile_start(j):
        if isinstance(j, int):
            return j * ATT_TILE
        return pl.multiple_of(j * ATT_TILE, ATT_TILE)

    def masked_queries(i):
        q = q_ref[pl.ds(tile_start(i), ATT_TILE), :]
        return [jnp.where((lane >= h * HEAD_DIM) & (lane < (h + 1) * HEAD_DIM), q,
                          jnp.zeros_like(q)) for h in heads]

    def window_start(i):
        return tile_start(max(i - 1, 0) if isinstance(i, int) else jnp.maximum(i - 1, 0))

    def block_rows(i):
        return pl.ds(tile_start(i), ATT_TILE)

    def put(bufs, h, tile, r0, c0, log_sig, log_fail):
        lf_ref, ls_ref, _ = bufs
        nr, nc = log_sig.shape
        lf_row = (2 * h + tile) * ATT_TILE + r0
        lf_ref[lf_row:lf_row + nr, c0:c0 + nc] = log_fail.astype(BF16)
        ls_ref[h * ATT_TILE + r0:h * ATT_TILE + r0 + nr,
               tile * ATT_TILE + c0:tile * ATT_TILE + c0 + nc] = log_sig
        return jnp.sum(log_fail, axis=-1, keepdims=True)

    def put_masked(bufs, h, tile, r0, c0, nr, nc):
        put(bufs, h, tile, r0, c0, jnp.full((nr, nc), MASKED_LOG2, F32), jnp.zeros((nr, nc), F32))

    def full_tile(bufs, h, tile, z):
        return put(bufs, h, tile, 0, 0, *_log2_terms(z))

    def diagonal_tile(bufs, h, tile, z):
        def masked(log_sig, log_fail):
            return jnp.where(causal, log_sig, MASKED_LOG2), jnp.where(causal, log_fail, 0.0)

        top = put(bufs, h, tile, 0, 0, *masked(*_log2_terms(z[:ATT_QUAD, :ATT_QUAD])))
        bottom = put(bufs, h, tile, ATT_QUAD, 0, *_log2_terms(z[ATT_QUAD:, :ATT_QUAD]))
        bottom = bottom + put(bufs, h, tile, ATT_QUAD, ATT_QUAD,
                              *masked(*_log2_terms(z[ATT_QUAD:, ATT_QUAD:])))
        return jnp.concatenate([top, bottom], axis=0)

    def scores(i, bufs, first=False):
        rs_ref = bufs[2]
        kk = k_ref[pl.ds(window_start(i), ATT_WINDOW), :]
        z = _dot_nt(jnp.concatenate(masked_queries(i), axis=0), kk)
        cmax = None
        for h in heads:
            zh = z[h * ATT_TILE:(h + 1) * ATT_TILE]
            if first:
                put_masked(bufs, h, 0, 0, ATT_QUAD, ATT_QUAD, ATT_QUAD)
                rs_old = diagonal_tile(bufs, h, 0, zh[:, :ATT_TILE])
                put_masked(bufs, h, 1, 0, 0, ATT_TILE, ATT_TILE)
                rs_new = jnp.zeros((ATT_TILE, 1), F32)
            else:
                rs_old = full_tile(bufs, h, 0, zh[:, :ATT_TILE])
                rs_new = diagonal_tile(bufs, h, 1, zh[:, ATT_TILE:])
            rs_ref[h] = rs_new
            carry = rs_old + rs_new
            carry_ref[i, h] = carry
            m = jnp.max(carry)
            cmax = m if cmax is None else jnp.maximum(cmax, m)
        cmax_ref[i] = cmax

    def weights(i, bufs):
        lf_ref, ls_ref, rs_ref = bufs
        vv = v_ref[pl.ds(window_start(i), ATT_WINDOW), :]
        later = _dot(lf_ref[...], upper_ref[...])
        ws = []
        for h in heads:
            q_rows = slice(h * ATT_TILE, (h + 1) * ATT_TILE)
            old = slice(2 * h * ATT_TILE, (2 * h + 1) * ATT_TILE)
            new = slice((2 * h + 1) * ATT_TILE, (2 * h + 2) * ATT_TILE)
            x_old = ls_ref[q_rows, 0:ATT_TILE] + (later[old] + rs_ref[h])
            x_new = ls_ref[q_rows, ATT_TILE:ATT_WINDOW] + later[new]
            ws.append(jnp.concatenate([jnp.exp2(x_old), jnp.exp2(x_new)], axis=1))
        pv = _dot(jnp.concatenate(ws, axis=0).astype(BF16), vv)
        o_ref[block_rows(i), :] = jnp.where(lane < HEAD_DIM, pv[:ATT_TILE], pv[ATT_TILE:])

    def finish(i, _):
        qms = masked_queries(i)

        def more(state):
            j, _, cmax = state
            return jnp.logical_and(j >= 0, cmax > F32_EXP2_ZERO)

        def sweep(state):
            j, carries, _ = state
            c0 = pl.multiple_of(j * ATT_TILE, ATT_TILE)
            kt = k_ref[pl.ds(c0, ATT_TILE), :]
            vt = v_ref[pl.ds(c0, ATT_TILE), :]
            new, pvs = [], []
            for h in heads:
                log_sig, log_fail = _log2_terms(_dot_nt(qms[h], kt))
                later = _dot(log_fail.astype(BF16), upper_ref[...])
                w = jnp.exp2(log_sig + later + carries[h])
                pvs.append(_dot(w.astype(BF16), vt))
                new.append(carries[h] + jnp.sum(log_fail, axis=-1, keepdims=True))
            o_ref[block_rows(i), :] += jnp.where(lane < HEAD_DIM, pvs[0], pvs[1])
            return j - 1, tuple(new), jnp.max(jnp.maximum(new[0], new[1]))

        carries = tuple(carry_ref[i, h] for h in heads)
        lax.while_loop(more, sweep, (jnp.maximum(i - 1, 0) - 1, carries, cmax_ref[i]))
        return 0

    def step(i, bufs, other_bufs):
        scores(i, bufs)
        weights(i - 1, other_bufs)

    def unrolled_steps(p, _):
        for k in range(ATT_UNROLL):
            step(ATT_UNROLL * p + 1 + k, sets[(1 + k) % 2], sets[k % 2])
        return 0

    for h in heads:
        put_masked(sets[1], h, 1, 0, ATT_QUAD, ATT_QUAD, ATT_QUAD)
    scores(0, sets[0], first=True)
    n_main = (n_blocks - 1) // ATT_UNROLL
    lax.fori_loop(0, n_main, unrolled_steps, 0)
    for i in range(ATT_UNROLL * n_main + 1, n_blocks):
        step(i, sets[i % 2], sets[(i - 1) % 2])
    weights(n_blocks - 1, sets[(n_blocks - 1) % 2])
    lax.fori_loop(0, n_blocks, finish, 0)


def _sb_attn(q, k, v, upper):
    b, s, _ = q.shape
    n_heads = LANES // HEAD_DIM
    blk = pl.BlockSpec((None, s, LANES), lambda bi, hp: (bi, 0, hp))
    return pl.pallas_call(
        _sb_attn_kernel,
        grid=(b, SB_WIDTH // LANES),
        in_specs=[blk, blk, blk,
                  pl.BlockSpec((ATT_TILE, ATT_TILE), lambda bi, hp: (0, 0))],
        out_specs=blk,
        out_shape=jax.ShapeDtypeStruct((b, s, SB_WIDTH), F32),
        scratch_shapes=2 * [
            pltpu.VMEM((n_heads * ATT_WINDOW, ATT_TILE), BF16),
            pltpu.VMEM((n_heads * ATT_TILE, ATT_WINDOW), F32),
            pltpu.VMEM((n_heads, ATT_TILE, 1), F32),
        ] + [
            pltpu.VMEM((s // ATT_TILE, n_heads, ATT_TILE, 1), F32),
            pltpu.SMEM((s // ATT_TILE,), F32),
        ],
        compiler_params=pltpu.CompilerParams(
            dimension_semantics=("arbitrary", "arbitrary"), vmem_limit_bytes=VMEM_LIMIT),
        name="sb_attn",
    )(q, k, v, upper)


def _mem_kv_kernel(mem_ref, g_ref, w_ref, kg_ref, ones_ref, mk_ref, mv_ref):
    x = mem_ref[...]
    ms = jnp.mean(x * x, axis=-1, keepdims=True)
    h = (x * lax.rsqrt(ms + EPS) * g_ref[...]).astype(BF16)
    kv = _dot(h, w_ref[...])
    mk = _head_rms(kv[:, :XA_WIDTH], ones_ref[...]) * kg_ref[...]
    mk_ref[...] = mk.astype(BF16)
    mv_ref[...] = kv[:, XA_WIDTH:].astype(BF16)


def _mem_kv(mem, mem_g, w_mem_kv, layer, kg, ones):
    b = mem.shape[0]
    const = lambda bi: (0, 0)
    per_b = lambda bi: (bi, 0, 0)
    return pl.pallas_call(
        _mem_kv_kernel,
        grid=(b,),
        in_specs=[
            pl.BlockSpec((None, N_MEM, D_MODEL), per_b),
            pl.BlockSpec((1, D_MODEL), const),
            pl.BlockSpec((None, D_MODEL, 2 * XA_WIDTH), lambda bi: (layer, 0, 0)),
            pl.BlockSpec((1, XA_WIDTH), const),
            pl.BlockSpec((MXU_WIDTH, MXU_WIDTH), const),
        ],
        out_specs=[pl.BlockSpec((None, N_MEM, XA_WIDTH), per_b),
                   pl.BlockSpec((None, N_MEM, XA_WIDTH), per_b)],
        out_shape=[jax.ShapeDtypeStruct((b, N_MEM, XA_WIDTH), BF16),
                   jax.ShapeDtypeStruct((b, N_MEM, XA_WIDTH), BF16)],
        compiler_params=pltpu.CompilerParams(
            dimension_semantics=("arbitrary",), vmem_limit_bytes=VMEM_LIMIT),
        name="mem_kv",
    )(mem, mem_g, w_mem_kv, kg, ones)


def _mix_out_kernel(x_ref, gates_ref, feats_ref, sb_ref, mk_ref, mv_ref, cw_ref, cb_ref,
                    wg_ref, bg_ref, lam_ref, xq_g_ref, ones_ref, wo_ref, o_ref, ext_ref,
                    h_ref):
    tm = x_ref.shape[0]

    @pl.when(pl.program_id(1) == 0)
    def _():
        ext_ref[0:SUBLANES, :] = jnp.zeros((SUBLANES, LRU_WIDTH), F32)
        h_ref[...] = jnp.zeros_like(h_ref)

    o_ref[...] = x_ref[...] + _dot(sb_ref[...].astype(BF16) * gates_ref[:, 0:SB_WIDTH],
                                   wo_ref[0:SB_WIDTH, :])
    lru_x = feats_ref[:, 0:LRU_WIDTH]
    xa_q = feats_ref[:, LRU_WIDTH:]

    ext_ref[SUBLANES:SUBLANES + tm, :] = lru_x
    xc = cb_ref[...] + cw_ref[CONV_WIDTH - 1:CONV_WIDTH, :] * lru_x
    for tap in range(CONV_WIDTH - 1):
        shift = CONV_WIDTH - 1 - tap
        xc = xc + cw_ref[tap:tap + 1, :] * ext_ref[SUBLANES - shift:SUBLANES - shift + tm, :]
    ext_ref[0:SUBLANES, :] = ext_ref[tm:tm + SUBLANES, :]

    gates = _dot(xc.astype(BF16), wg_ref[...]) + bg_ref[...]
    r = _sigmoid(gates[:, :LRU_WIDTH])
    i_gate = _sigmoid(gates[:, LRU_WIDTH:])
    neg_lam = -lam_ref[...]
    softplus_neg_lam = jnp.maximum(neg_lam, 0.0) + jnp.log(1.0 + jnp.exp(-jnp.abs(neg_lam)))
    log_a = (-LRU_C) * r * softplus_neg_lam
    a = jnp.exp(log_a)
    u = jnp.sqrt(1.0 - jnp.exp(2.0 * log_a)) * (i_gate * xc)
    n_groups = tm // SUBLANES
    a = a.reshape(n_groups, SUBLANES, LRU_WIDTH)
    u = u.reshape(n_groups, SUBLANES, LRU_WIDTH)
    sub = lax.broadcasted_iota(jnp.int32, (1, SUBLANES, LRU_WIDTH), 1)
    d = 1
    while d < SUBLANES:
        keep = sub >= d
        a_prev = jnp.where(keep, pltpu.roll(a, d, 1), 1.0)
        u_prev = jnp.where(keep, pltpu.roll(u, d, 1), 0.0)
        u = u + a * u_prev
        a = a * a_prev
        d *= 2
    h_in = h_ref[...]
    h_before = []
    for g in range(n_groups):
        h_before.append(h_in)
        a_tot = jnp.broadcast_to(a[g, SUBLANES - 1:SUBLANES, :], (SUBLANES, LRU_WIDTH))
        u_tot = jnp.broadcast_to(u[g, SUBLANES - 1:SUBLANES, :], (SUBLANES, LRU_WIDTH))
        h_in = a_tot * h_in + u_tot
    h_ref[...] = h_in
    h = (u + a * jnp.stack(h_before, axis=0)).reshape(tm, LRU_WIDTH)

    lane = lax.broadcasted_iota(jnp.int32, (1, XA_WIDTH), 1)
    qn = _head_rms(xa_q, ones_ref[...]) * xq_g_ref[...]
    mk = mk_ref[...]
    mv = mv_ref[...]
    xa = jnp.zeros((tm, XA_WIDTH), F32)
    for head in range(XA_WIDTH // HEAD_DIM):
        in_head = (lane >= head * HEAD_DIM) & (lane < (head + 1) * HEAD_DIM)
        qh = jnp.where(in_head, qn, 0.0).astype(BF16)
        s = _dot_nt(qh, mk)
        p = jnp.exp2(s - jnp.max(s, axis=-1, keepdims=True))
        denom = jnp.sum(p, axis=-1, keepdims=True)
        oh = _dot(p.astype(BF16), mv)
        xa = jnp.where(in_head, oh / denom, xa)

    y = jnp.concatenate([h, xa], axis=1).astype(BF16) * gates_ref[:, SB_WIDTH:]
    o_ref[...] += _dot(y, wo_ref[SB_WIDTH:, :])


def _mix_out(x, gates, feats, sb, mk, mv, conv_w, conv_b, w_gates, b_gates, lam, xq_g, ones,
             w_out, layer):
    b, s, _ = x.shape
    tile = lambda w: pl.BlockSpec((None, ROW_TILE, w), lambda bi, si: (bi, si, 0))
    per_b = lambda r, w: pl.BlockSpec((None, r, w), lambda bi, si: (bi, 0, 0))
    const = lambda r, w: pl.BlockSpec((r, w), lambda bi, si: (0, 0))
    return pl.pallas_call(
        _mix_out_kernel,
        grid=(b, s // ROW_TILE),
        in_specs=[
            tile(D_MODEL), tile(D_MIX), tile(LRU_WIDTH + XA_WIDTH), tile(SB_WIDTH),
            per_b(N_MEM, XA_WIDTH), per_b(N_MEM, XA_WIDTH),
            const(CONV_WIDTH, LRU_WIDTH), const(1, LRU_WIDTH),
            const(LRU_WIDTH, 2 * LRU_WIDTH), const(1, 2 * LRU_WIDTH),
            const(1, LRU_WIDTH), const(1, XA_WIDTH),
            const(MXU_WIDTH, MXU_WIDTH),
            pl.BlockSpec((None, D_MIX, D_MODEL), lambda bi, si: (layer, 0, 0)),
        ],
        out_specs=tile(D_MODEL),
        out_shape=jax.ShapeDtypeStruct((b, s, D_MODEL), F32),
        scratch_shapes=[pltpu.VMEM((ROW_TILE + SUBLANES, LRU_WIDTH), F32),
                        pltpu.VMEM((SUBLANES, LRU_WIDTH), F32)],
        compiler_params=pltpu.CompilerParams(
            dimension_semantics=("arbitrary", "arbitrary"), vmem_limit_bytes=VMEM_LIMIT),
        name="mix_out",
    )(x, gates, feats, sb, mk, mv, conv_w, conv_b, w_gates, b_gates, lam, xq_g, ones, w_out)


def _block_diag(w):
    n, d, _ = w.shape
    eye = jnp.eye(n, dtype=w.dtype)
    return (eye[:, None, :, None] * w[:, :, None, :]).reshape(n * d, n * d)


def kernel(x, mem, norm_g, w_in, sb_q_g, sb_k_g, conv_w, conv_b, w_rg, b_rg, w_ig, b_ig,
           lru_lambda, xa_q_g, xa_k_g, mem_g, w_mem_kv, w_out):
    b, s, d = x.shape
    depth = norm_g.shape[0]
    scale = HEAD_DIM ** -0.5
    idx = jnp.arange(MXU_WIDTH)
    group_ones = (idx[:, None] // HEAD_DIM == idx[None, :] // HEAD_DIM).astype(BF16)
    upper = (idx[:, None] > idx[None, :]).astype(BF16)
    w_in, w_mem_kv, w_out = (w.astype(BF16) for w in (w_in, w_mem_kv, w_out))

    for l in range(depth):
        qg = (jnp.tile(sb_q_g[l], SB_WIDTH // HEAD_DIM) * (scale * LOG2_E)).reshape(1, SB_WIDTH)
        kg = jnp.tile(sb_k_g[l], SB_WIDTH // HEAD_DIM).reshape(1, SB_WIDTH)
        q, k, v, gates, feats = _in_proj(x.reshape(b * s, d), norm_g[l].reshape(1, d),
                                         w_in, l, qg, kg, group_ones)
        sb = _sb_attn(q.reshape(b, s, SB_WIDTH), k.reshape(b, s, SB_WIDTH),
                      v.reshape(b, s, SB_WIDTH), upper)
        mkg = jnp.tile(xa_k_g[l], XA_WIDTH // HEAD_DIM).reshape(1, XA_WIDTH)
        mk, mv = _mem_kv(mem, mem_g[l].reshape(1, d), w_mem_kv, l, mkg, group_ones)
        w_gates = jnp.concatenate([_block_diag(w_rg[l]), _block_diag(w_ig[l])],
                                  axis=1).astype(BF16)
        b_gates = jnp.concatenate([b_rg[l], b_ig[l]]).reshape(1, 2 * LRU_WIDTH)
        xq_g = (jnp.tile(xa_q_g[l], XA_WIDTH // HEAD_DIM) * (scale * LOG2_E)).reshape(1, XA_WIDTH)
        x = _mix_out(x, gates.reshape(b, s, D_MIX), feats.reshape(b, s, LRU_WIDTH + XA_WIDTH),
                     sb, mk, mv, conv_w[l], conv_b[l].reshape(1, LRU_WIDTH), w_gates, b_gates,
                     lru_lambda[l].reshape(1, LRU_WIDTH), xq_g, group_ones, w_out, l)
    return x
```

```python
import math

import jax
import jax.numpy as jnp
from jax import lax
from jax.experimental import pallas as pl
from jax.experimental.pallas import tpu as pltpu

D_MODEL = 1024
HEAD_DIM = 64
SB_WIDTH = 512
LRU_WIDTH = 256
XA_WIDTH = 256
N_MEM = 256
CONV_WIDTH = 4
LRU_C = 8.0
EPS = 1e-6
D_IN = 4 * SB_WIDTH + 2 * LRU_WIDTH + 2 * XA_WIDTH
D_MIX = SB_WIDTH + LRU_WIDTH + XA_WIDTH

MXU_WIDTH = 256
LANES = 128
SUBLANES = 8
ROW_TILE = 1024
ATT_TILE = MXU_WIDTH
ATT_QUAD = ATT_TILE // 2
ATT_WINDOW = 2 * ATT_TILE
ATT_UNROLL = 4
VMEM_LIMIT = 48 * 1024 * 1024

LOG2_E = math.log2(math.e)
F32_EXP2_ZERO = -150.0
MASKED_LOG2 = -1e30

F32 = jnp.float32
BF16 = jnp.bfloat16


def _dot(a, b):
    return jnp.dot(a, b, preferred_element_type=F32)


def _dot_nt(a, b):
    return lax.dot_general(a, b, (((1,), (1,)), ((), ())), preferred_element_type=F32)


def _head_rms(t, group_ones):
    outs = []
    for j in range(t.shape[1] // MXU_WIDTH):
        tj = t[:, j * MXU_WIDTH:(j + 1) * MXU_WIDTH]
        ss = _dot((tj * tj).astype(BF16), group_ones)
        outs.append(tj * lax.rsqrt(ss * (1.0 / HEAD_DIM) + EPS))
    return outs[0] if len(outs) == 1 else jnp.concatenate(outs, axis=1)


def _sigmoid(x):
    return 0.5 * jnp.tanh(0.5 * x) + 0.5


def _silu(x):
    half = 0.5 * x
    return half * jnp.tanh(half) + half


def _cast_weight_once(w_ref, wb_ref, first_step):
    @pl.when(first_step)
    def _():
        for c in range(0, w_ref.shape[1], 2 * MXU_WIDTH):
            wb_ref[:, c:c + 2 * MXU_WIDTH] = w_ref[:, c:c + 2 * MXU_WIDTH].astype(BF16)


def _in_proj_kernel(x_ref, g_ref, w_ref, qg_ref, kg_ref, ones_ref,
                    q_ref, k_ref, v_ref, gates_ref, feats_ref, wb_ref):
    _cast_weight_once(w_ref, wb_ref, pl.program_id(0) == 0)
    x = x_ref[...]
    ms = jnp.mean(x * x, axis=-1, keepdims=True)
    h = (x * lax.rsqrt(ms + EPS) * g_ref[...]).astype(BF16)
    ones = ones_ref[...]

    def proj(c0, width):
        return _dot(h, wb_ref[:, c0:c0 + width])

    q_ref[...] = (_head_rms(proj(0, SB_WIDTH), ones) * qg_ref[...]).astype(BF16)
    k_ref[...] = (_head_rms(proj(SB_WIDTH, SB_WIDTH), ones) * kg_ref[...]).astype(BF16)
    v_ref[...] = proj(2 * SB_WIDTH, SB_WIDTH).astype(BF16)
    gates_ref[:, 0:SB_WIDTH] = _silu(proj(3 * SB_WIDTH, SB_WIDTH)).astype(BF16)
    c0 = 4 * SB_WIDTH
    for g, width in enumerate((LRU_WIDTH, XA_WIDTH)):
        pair = proj(c0, 2 * width)
        feats_ref[:, g * LRU_WIDTH:g * LRU_WIDTH + width] = pair[:, :width]
        gates_ref[:, SB_WIDTH + g * LRU_WIDTH:SB_WIDTH + g * LRU_WIDTH + width] = (
            _silu(pair[:, width:]).astype(BF16))
        c0 += 2 * width


def _in_proj(x2d, norm_g, w_in, layer, qg, kg, ones):
    m = x2d.shape[0]
    row = lambda i: (i, 0)
    const = lambda i: (0, 0)
    return pl.pallas_call(
        _in_proj_kernel,
        grid=(m // ROW_TILE,),
        in_specs=[
            pl.BlockSpec((ROW_TILE, D_MODEL), row),
            pl.BlockSpec((1, D_MODEL), const),
            pl.BlockSpec((None, D_MODEL, D_IN), lambda i: (layer, 0, 0),
                         pipeline_mode=pl.Buffered(1)),
            pl.BlockSpec((1, SB_WIDTH), const),
            pl.BlockSpec((1, SB_WIDTH), const),
            pl.BlockSpec((MXU_WIDTH, MXU_WIDTH), const),
        ],
        out_specs=[
            pl.BlockSpec((ROW_TILE, SB_WIDTH), row),
            pl.BlockSpec((ROW_TILE, SB_WIDTH), row),
            pl.BlockSpec((ROW_TILE, SB_WIDTH), row),
            pl.BlockSpec((ROW_TILE, D_MIX), row),
            pl.BlockSpec((ROW_TILE, LRU_WIDTH + XA_WIDTH), row),
        ],
        out_shape=[
            jax.ShapeDtypeStruct((m, SB_WIDTH), BF16),
            jax.ShapeDtypeStruct((m, SB_WIDTH), BF16),
            jax.ShapeDtypeStruct((m, SB_WIDTH), BF16),
            jax.ShapeDtypeStruct((m, D_MIX), BF16),
            jax.ShapeDtypeStruct((m, LRU_WIDTH + XA_WIDTH), F32),
        ],
        scratch_shapes=[pltpu.VMEM((D_MODEL, D_IN), BF16)],
        compiler_params=pltpu.CompilerParams(
            dimension_semantics=("arbitrary",), vmem_limit_bytes=VMEM_LIMIT),
        name="in_proj",
    )(x2d, norm_g, w_in, qg, kg, ones)


def _log2_terms(z):
    t = jnp.log2(1.0 + jnp.exp2(-jnp.abs(z)))
    log_sig = jnp.minimum(z, 0.0) - t
    return log_sig, log_sig - z


def _sb_attn_kernel(q_ref, k_ref, v_ref, upper_ref, o_ref, *scratch):
    seq = q_ref.shape[0]
    n_blocks = seq // ATT_TILE
    assert ATT_UNROLL % 2 == 0 and n_blocks >= 3
    heads = range(LANES // HEAD_DIM)
    sets = (scratch[0:3], scratch[3:6])
    carry_ref, cmax_ref = scratch[6:8]
    lane = lax.broadcasted_iota(jnp.int32, (1, LANES), 1)
    causal = (lax.broadcasted_iota(jnp.int32, (ATT_QUAD, ATT_QUAD), 1)
              < lax.broadcasted_iota(jnp.int32, (ATT_QUAD, ATT_QUAD), 0))

    def tile_start(j):
        if isinstance(j, int):
            return j * ATT_TILE
        return pl.multiple_of(j * ATT_TILE, ATT_TILE)

    def masked_queries(i):
        q = q_ref[pl.ds(tile_start(i), ATT_TILE), :]
        return [jnp.where((lane >= h * HEAD_DIM) & (lane < (h + 1) * HEAD_DIM), q,
                          jnp.zeros_like(q)) for h in heads]

    def window_start(i):
        return tile_start(max(i - 1, 0) if isinstance(i, int) else jnp.maximum(i - 1, 0))

    def block_rows(i):
        return pl.ds(tile_start(i), ATT_TILE)

    def put(bufs, h, tile, r0, c0, log_sig, log_fail):
        lf_ref, ls_ref, _ = bufs
        nr, nc = log_sig.shape
        lf_row = (2 * h + tile) * ATT_TILE + r0
        lf_ref[lf_row:lf_row + nr, c0:c0 + nc] = log_fail.astype(BF16)
        ls_ref[h * ATT_TILE + r0:h * ATT_TILE + r0 + nr,
               tile * ATT_TILE + c0:tile * ATT_TILE + c0 + nc] = log_sig
        return jnp.sum(log_fail, axis=-1, keepdims=True)

    def put_masked(bufs, h, tile, r0, c0, nr, nc):
        put(bufs, h, tile, r0, c0, jnp.full((nr, nc), MASKED_LOG2, F32), jnp.zeros((nr, nc), F32))

    def full_tile(bufs, h, tile, z):
        return put(bufs, h, tile, 0, 0, *_log2_terms(z))

    def diagonal_tile(bufs, h, tile, z):
        def masked(log_sig, log_fail):
            return jnp.where(causal, log_sig, MASKED_LOG2), jnp.where(causal, log_fail, 0.0)

        top = put(bufs, h, tile, 0, 0, *masked(*_log2_terms(z[:ATT_QUAD, :ATT_QUAD])))
        bottom = put(bufs, h, tile, ATT_QUAD, 0, *_log2_terms(z[ATT_QUAD:, :ATT_QUAD]))
        bottom = bottom + put(bufs, h, tile, ATT_QUAD, ATT_QUAD,
                              *masked(*_log2_terms(z[ATT_QUAD:, ATT_QUAD:])))
        return jnp.concatenate([top, bottom], axis=0)

    def scores(i, bufs, first=False):
        rs_ref = bufs[2]
        kk = k_ref[pl.ds(window_start(i), ATT_WINDOW), :]
        z = _dot_nt(jnp.concatenate(masked_queries(i), axis=0), kk)
        cmax = None
        for h in heads:
            zh = z[h * ATT_TILE:(h + 1) * ATT_TILE]
            if first:
                put_masked(bufs, h, 0, 0, ATT_QUAD, ATT_QUAD, ATT_QUAD)
                rs_old = diagonal_tile(bufs, h, 0, zh[:, :ATT_TILE])
                put_masked(bufs, h, 1, 0, 0, ATT_TILE, ATT_TILE)
                rs_new = jnp.zeros((ATT_TILE, 1), F32)
            else:
                rs_old = full_tile(bufs, h, 0, zh[:, :ATT_TILE])
                rs_new = diagonal_tile(bufs, h, 1, zh[:, ATT_TILE:])
            rs_ref[h] = rs_new
            carry = rs_old + rs_new
            carry_ref[i, h] = carry
            m = jnp.max(carry)
            cmax = m if cmax is None else jnp.maximum(cmax, m)
        cmax_ref[i] = cmax

    def weights(i, bufs):
        lf_ref, ls_ref, rs_ref = bufs
        vv = v_ref[pl.ds(window_start(i), ATT_WINDOW), :]
        later = _dot(lf_ref[...], upper_ref[...])
        ws = []
        for h in heads:
            q_rows = slice(h * ATT_TILE, (h + 1) * ATT_TILE)
            old = slice(2 * h * ATT_TILE, (2 * h + 1) * ATT_TILE)
            new = slice((2 * h + 1) * ATT_TILE, (2 * h + 2) * ATT_TILE)
            x_old = ls_ref[q_rows, 0:ATT_TILE] + (later[old] + rs_ref[h])
            x_new = ls_ref[q_rows, ATT_TILE:ATT_WINDOW] + later[new]
            ws.append(jnp.concatenate([jnp.exp2(x_old), jnp.exp2(x_new)], axis=1))
        pv = _dot(jnp.concatenate(ws, axis=0).astype(BF16), vv)
        o_ref[block_rows(i), :] = jnp.where(lane < HEAD_DIM, pv[:ATT_TILE], pv[ATT_TILE:])

    def finish(i, _):
        qms = masked_queries(i)

        def more(state):
            j, _, cmax = state
            return jnp.logical_and(j >= 0, cmax > F32_EXP2_ZERO)

        def sweep(state):
            j, carries, _ = state
            c0 = pl.multiple_of(j * ATT_TILE, ATT_TILE)
            kt = k_ref[pl.ds(c0, ATT_TILE), :]
            vt = v_ref[pl.ds(c0, ATT_TILE), :]
            new, pvs = [], []
            for h in heads:
                log_sig, log_fail = _log2_terms(_dot_nt(qms[h], kt))
                later = _dot(log_fail.astype(BF16), upper_ref[...])
                w = jnp.exp2(log_sig + later + carries[h])
                pvs.append(_dot(w.astype(BF16), vt))
                new.append(carries[h] + jnp.sum(log_fail, axis=-1, keepdims=True))
            o_ref[block_rows(i), :] += jnp.where(lane < HEAD_DIM, pvs[0], pvs[1])
            return j - 1, tuple(new), jnp.max(jnp.maximum(new[0], new[1]))

        carries = tuple(carry_ref[i, h] for h in heads)
        lax.while_loop(more, sweep, (jnp.maximum(i - 1, 0) - 1, carries, cmax_ref[i]))
        return 0

    def step(i, bufs, other_bufs):
        scores(i, bufs)
        weights(i - 1, other_bufs)

    def unrolled_steps(p, _):
        for k in range(ATT_UNROLL):
            step(ATT_UNROLL * p + 1 + k, sets[(1 + k) % 2], sets[k % 2])
        return 0

    for h in heads:
        put_masked(sets[1], h, 1, 0, ATT_QUAD, ATT_QUAD, ATT_QUAD)
    scores(0, sets[0], first=True)
    n_main = (n_blocks - 1) // ATT_UNROLL
    lax.fori_loop(0, n_main, unrolled_steps, 0)
    for i in range(ATT_UNROLL * n_main + 1, n_blocks):
        step(i, sets[i % 2], sets[(i - 1) % 2])
    weights(n_blocks - 1, sets[(n_blocks - 1) % 2])

    worst = cmax_ref[2]
    for i in range(3, n_blocks):
        worst = jnp.maximum(worst, cmax_ref[i])

    @pl.when(worst > F32_EXP2_ZERO)
    def _():
        lax.fori_loop(2, n_blocks, finish, 0)


def _sb_attn(q, k, v, upper):
    b, s, _ = q.shape
    n_heads = LANES // HEAD_DIM
    blk = pl.BlockSpec((None, s, LANES), lambda bi, hp: (bi, 0, hp))
    return pl.pallas_call(
        _sb_attn_kernel,
        grid=(b, SB_WIDTH // LANES),
        in_specs=[blk, blk, blk,
                  pl.BlockSpec((ATT_TILE, ATT_TILE), lambda bi, hp: (0, 0))],
        out_specs=blk,
        out_shape=jax.ShapeDtypeStruct((b, s, SB_WIDTH), F32),
        scratch_shapes=2 * [
            pltpu.VMEM((n_heads * ATT_WINDOW, ATT_TILE), BF16),
            pltpu.VMEM((n_heads * ATT_TILE, ATT_WINDOW), F32),
            pltpu.VMEM((n_heads, ATT_TILE, 1), F32),
        ] + [
            pltpu.VMEM((s // ATT_TILE, n_heads, ATT_TILE, 1), F32),
            pltpu.SMEM((s // ATT_TILE,), F32),
        ],
        compiler_params=pltpu.CompilerParams(
            dimension_semantics=("arbitrary", "arbitrary"), vmem_limit_bytes=VMEM_LIMIT),
        name="sb_attn",
    )(q, k, v, upper)


def _mem_kv_kernel(mem_ref, g_ref, w_ref, kg_ref, ones_ref, mk_ref, mv_ref):
    x = mem_ref[...]
    ms = jnp.mean(x * x, axis=-1, keepdims=True)
    h = (x * lax.rsqrt(ms + EPS) * g_ref[...]).astype(BF16)
    kv = _dot(h, w_ref[...].astype(BF16))
    mk = _head_rms(kv[:, :XA_WIDTH], ones_ref[...]) * kg_ref[...]
    mk_ref[...] = mk.astype(BF16)
    mv_ref[...] = kv[:, XA_WIDTH:].astype(BF16)


def _mem_kv(mem, mem_g, w_mem_kv, layer, kg, ones):
    b = mem.shape[0]
    const = lambda bi: (0, 0)
    per_b = lambda bi: (bi, 0, 0)
    return pl.pallas_call(
        _mem_kv_kernel,
        grid=(b,),
        in_specs=[
            pl.BlockSpec((None, N_MEM, D_MODEL), per_b),
            pl.BlockSpec((1, D_MODEL), const),
            pl.BlockSpec((None, D_MODEL, 2 * XA_WIDTH), lambda bi: (layer, 0, 0)),
            pl.BlockSpec((1, XA_WIDTH), const),
            pl.BlockSpec((MXU_WIDTH, MXU_WIDTH), const),
        ],
        out_specs=[pl.BlockSpec((None, N_MEM, XA_WIDTH), per_b),
                   pl.BlockSpec((None, N_MEM, XA_WIDTH), per_b)],
        out_shape=[jax.ShapeDtypeStruct((b, N_MEM, XA_WIDTH), BF16),
                   jax.ShapeDtypeStruct((b, N_MEM, XA_WIDTH), BF16)],
        compiler_params=pltpu.CompilerParams(
            dimension_semantics=("arbitrary",), vmem_limit_bytes=VMEM_LIMIT),
        name="mem_kv",
    )(mem, mem_g, w_mem_kv, kg, ones)


def _mix_out_kernel(x_ref, gates_ref, feats_ref, sb_ref, mk_ref, mv_ref, cw_ref, cb_ref,
                    wg_ref, bg_ref, lam_ref, xq_g_ref, ones_ref, wo32_ref, o_ref, ext_ref,
                    h_ref, wo_ref):
    tm = x_ref.shape[0]
    _cast_weight_once(wo32_ref, wo_ref, (pl.program_id(0) == 0) & (pl.program_id(1) == 0))

    @pl.when(pl.program_id(1) == 0)
    def _():
        ext_ref[0:SUBLANES, :] = jnp.zeros((SUBLANES, LRU_WIDTH), F32)
        h_ref[...] = jnp.zeros_like(h_ref)

    o_ref[...] = x_ref[...] + _dot(sb_ref[...].astype(BF16) * gates_ref[:, 0:SB_WIDTH],
                                   wo_ref[0:SB_WIDTH, :])
    lru_x = feats_ref[:, 0:LRU_WIDTH]
    xa_q = feats_ref[:, LRU_WIDTH:]

    ext_ref[SUBLANES:SUBLANES + tm, :] = lru_x
    xc = cb_ref[...] + cw_ref[CONV_WIDTH - 1:CONV_WIDTH, :] * lru_x
    for tap in range(CONV_WIDTH - 1):
        shift = CONV_WIDTH - 1 - tap
        xc = xc + cw_ref[tap:tap + 1, :] * ext_ref[SUBLANES - shift:SUBLANES - shift + tm, :]
    ext_ref[0:SUBLANES, :] = ext_ref[tm:tm + SUBLANES, :]

    gates = _dot(xc.astype(BF16), wg_ref[...]) + bg_ref[...]
    r = _sigmoid(gates[:, :LRU_WIDTH])
    i_gate = _sigmoid(gates[:, LRU_WIDTH:])
    neg_lam = -lam_ref[...]
    softplus_neg_lam = jnp.maximum(neg_lam, 0.0) + jnp.log(1.0 + jnp.exp(-jnp.abs(neg_lam)))
    log_a = (-LRU_C) * r * softplus_neg_lam
    a = jnp.exp(log_a)
    one_minus_a2 = 1.0 - jnp.exp(2.0 * log_a)
    root = jnp.where(one_minus_a2 > 0.0, one_minus_a2 * lax.rsqrt(one_minus_a2), 0.0)
    u = root * (i_gate * xc)
    n_groups = tm // SUBLANES
    a = a.reshape(n_groups, SUBLANES, LRU_WIDTH)
    u = u.reshape(n_groups, SUBLANES, LRU_WIDTH)
    sub = lax.broadcasted_iota(jnp.int32, (1, SUBLANES, LRU_WIDTH), 1)
    d = 1
    while d < SUBLANES:
        keep = sub >= d
        a_prev = jnp.where(keep, pltpu.roll(a, d, 1), 1.0)
        u_prev = jnp.where(keep, pltpu.roll(u, d, 1), 0.0)
        u = u + a * u_prev
        a = a * a_prev
        d *= 2
    h_in = h_ref[...]
    h_before = []
    for g in range(n_groups):
        h_before.append(h_in)
        a_tot = jnp.broadcast_to(a[g, SUBLANES - 1:SUBLANES, :], (SUBLANES, LRU_WIDTH))
        u_tot = jnp.broadcast_to(u[g, SUBLANES - 1:SUBLANES, :], (SUBLANES, LRU_WIDTH))
        h_in = a_tot * h_in + u_tot
    h_ref[...] = h_in
    h = (u + a * jnp.stack(h_before, axis=0)).reshape(tm, LRU_WIDTH)

    lane = lax.broadcasted_iota(jnp.int32, (1, XA_WIDTH), 1)
    qn = _head_rms(xa_q, ones_ref[...]) * xq_g_ref[...]
    mk = mk_ref[...]
    mv = mv_ref[...]
    xa = jnp.zeros((tm, XA_WIDTH), F32)
    for head in range(XA_WIDTH // HEAD_DIM):
        in_head = (lane >= head * HEAD_DIM) & (lane < (head + 1) * HEAD_DIM)
        qh = jnp.where(in_head, qn, 0.0).astype(BF16)
        s = _dot_nt(qh, mk)
        p = jnp.exp2(s - jnp.max(s, axis=-1, keepdims=True))
        denom = jnp.sum(p, axis=-1, keepdims=True)
        oh = _dot(p.astype(BF16), mv)
        xa = jnp.where(in_head, oh / denom, xa)

    y = jnp.concatenate([h, xa], axis=1).astype(BF16) * gates_ref[:, SB_WIDTH:]
    o_ref[...] += _dot(y, wo_ref[SB_WIDTH:, :])


def _mix_out(x, gates, feats, sb, mk, mv, conv_w, conv_b, w_gates, b_gates, lam, xq_g, ones,
             w_out, layer):
    b, s, _ = x.shape
    tile = lambda w: pl.BlockSpec((None, ROW_TILE, w), lambda bi, si: (bi, si, 0))
    per_b = lambda r, w: pl.BlockSpec((None, r, w), lambda bi, si: (bi, 0, 0))
    const = lambda r, w: pl.BlockSpec((r, w), lambda bi, si: (0, 0))
    return pl.pallas_call(
        _mix_out_kernel,
        grid=(b, s // ROW_TILE),
        in_specs=[
            tile(D_MODEL), tile(D_MIX), tile(LRU_WIDTH + XA_WIDTH), tile(SB_WIDTH),
            per_b(N_MEM, XA_WIDTH), per_b(N_MEM, XA_WIDTH),
            const(CONV_WIDTH, LRU_WIDTH), const(1, LRU_WIDTH),
            const(LRU_WIDTH, 2 * LRU_WIDTH), const(1, 2 * LRU_WIDTH),
            const(1, LRU_WIDTH), const(1, XA_WIDTH),
            const(MXU_WIDTH, MXU_WIDTH),
            pl.BlockSpec((None, D_MIX, D_MODEL), lambda bi, si: (layer, 0, 0),
                         pipeline_mode=pl.Buffered(1)),
        ],
        out_specs=tile(D_MODEL),
        out_shape=jax.ShapeDtypeStruct((b, s, D_MODEL), F32),
        scratch_shapes=[pltpu.VMEM((ROW_TILE + SUBLANES, LRU_WIDTH), F32),
                        pltpu.VMEM((SUBLANES, LRU_WIDTH), F32),
                        pltpu.VMEM((D_MIX, D_MODEL), BF16)],
        compiler_params=pltpu.CompilerParams(
            dimension_semantics=("arbitrary", "arbitrary"), vmem_limit_bytes=VMEM_LIMIT),
        name="mix_out",
    )(x, gates, feats, sb, mk, mv, conv_w, conv_b, w_gates, b_gates, lam, xq_g, ones, w_out)


def _block_diag(w):
    n, d, _ = w.shape
    eye = jnp.eye(n, dtype=w.dtype)
    return (eye[:, None, :, None] * w[:, :, None, :]).reshape(n * d, n * d)


def kernel(x, mem, norm_g, w_in, sb_q_g, sb_k_g, conv_w, conv_b, w_rg, b_rg, w_ig, b_ig,
           lru_lambda, xa_q_g, xa_k_g, mem_g, w_mem_kv, w_out):
    b, s, d = x.shape
    depth = norm_g.shape[0]
    scale = HEAD_DIM ** -0.5
    idx = jnp.arange(MXU_WIDTH)
    group_ones = (idx[:, None] // HEAD_DIM == idx[None, :] // HEAD_DIM).astype(BF16)
    upper = (idx[:, None] > idx[None, :]).astype(BF16)

    for l in range(depth):
        qg = (jnp.tile(sb_q_g[l], SB_WIDTH // HEAD_DIM) * (scale * LOG2_E)).reshape(1, SB_WIDTH)
        kg = jnp.tile(sb_k_g[l], SB_WIDTH // HEAD_DIM).reshape(1, SB_WIDTH)
        q, k, v, gates, feats = _in_proj(x.reshape(b * s, d), norm_g[l].reshape(1, d),
                                         w_in, l, qg, kg, group_ones)
        sb = _sb_attn(q.reshape(b, s, SB_WIDTH), k.reshape(b, s, SB_WIDTH),
                      v.reshape(b, s, SB_WIDTH), upper)
        mkg = jnp.tile(xa_k_g[l], XA_WIDTH // HEAD_DIM).reshape(1, XA_WIDTH)
        mk, mv = _mem_kv(mem, mem_g[l].reshape(1, d), w_mem_kv, l, mkg, group_ones)
        w_gates = jnp.concatenate([_block_diag(w_rg[l]), _block_diag(w_ig[l])],
                                  axis=1).astype(BF16)
        b_gates = jnp.concatenate([b_rg[l], b_ig[l]]).reshape(1, 2 * LRU_WIDTH)
        xq_g = (jnp.tile(xa_q_g[l], XA_WIDTH // HEAD_DIM) * (scale * LOG2_E)).reshape(1, XA_WIDTH)
        x = _mix_out(x, gates.reshape(b, s, D_MIX), feats.reshape(b, s, LRU_WIDTH + XA_WIDTH),
                     sb, mk, mv, conv_w[l], conv_b[l].reshape(1, LRU_WIDTH), w_gates, b_gates,
                     lru_lambda[l].reshape(1, LRU_WIDTH), xq_g, group_ones, w_out, l)
    return x
```

```python
import math

import jax
import jax.numpy as jnp
from jax import lax
from jax.experimental import pallas as pl
from jax.experimental.pallas import tpu as pltpu

D_MODEL = 1024
HEAD_DIM = 64
SB_WIDTH = 512
LRU_WIDTH = 256
XA_WIDTH = 256
N_MEM = 256
CONV_WIDTH = 4
LRU_C = 8.0
EPS = 1e-6
D_IN = 4 * SB_WIDTH + 2 * LRU_WIDTH + 2 * XA_WIDTH
D_MIX = SB_WIDTH + LRU_WIDTH + XA_WIDTH

MXU_WIDTH = 256
LANES = 128
SUBLANES = 8
ROW_TILE = 1024
ATT_TILE = MXU_WIDTH
ATT_QUAD = ATT_TILE // 2
ATT_WINDOW = 2 * ATT_TILE
ATT_UNROLL = 14
VMEM_LIMIT = 48 * 1024 * 1024

LOG2_E = math.log2(math.e)
F32_EXP2_ZERO = -150.0
MASKED_LOG2 = -1e30

F32 = jnp.float32
BF16 = jnp.bfloat16


def _dot(a, b):
    return jnp.dot(a, b, preferred_element_type=F32)


def _dot_nt(a, b):
    return lax.dot_general(a, b, (((1,), (1,)), ((), ())), preferred_element_type=F32)


def _head_rms(t, group_ones):
    outs = []
    for j in range(t.shape[1] // MXU_WIDTH):
        tj = t[:, j * MXU_WIDTH:(j + 1) * MXU_WIDTH]
        ss = _dot((tj * tj).astype(BF16), group_ones)
        outs.append(tj * lax.rsqrt(ss * (1.0 / HEAD_DIM) + EPS))
    return outs[0] if len(outs) == 1 else jnp.concatenate(outs, axis=1)


def _sigmoid(x):
    return 0.5 * jnp.tanh(0.5 * x) + 0.5


def _silu(x):
    half = 0.5 * x
    return half * jnp.tanh(half) + half


def _cast_weight_once(w_ref, wb_ref, first_step):
    @pl.when(first_step)
    def _():
        for c in range(0, w_ref.shape[1], 2 * MXU_WIDTH):
            wb_ref[:, c:c + 2 * MXU_WIDTH] = w_ref[:, c:c + 2 * MXU_WIDTH].astype(BF16)


def _in_proj_kernel(x_ref, g_ref, w_ref, qg_ref, kg_ref, ones_ref,
                    q_ref, k_ref, v_ref, gates_ref, feats_ref, wb_ref):
    _cast_weight_once(w_ref, wb_ref, pl.program_id(0) == 0)
    x = x_ref[...]
    ms = jnp.mean(x * x, axis=-1, keepdims=True)
    h = (x * lax.rsqrt(ms + EPS) * g_ref[...]).astype(BF16)
    ones = ones_ref[...]

    def proj(c0, width):
        return _dot(h, wb_ref[:, c0:c0 + width])

    q_ref[...] = (_head_rms(proj(0, SB_WIDTH), ones) * qg_ref[...]).astype(BF16)
    k_ref[...] = (_head_rms(proj(SB_WIDTH, SB_WIDTH), ones) * kg_ref[...]).astype(BF16)
    v_ref[...] = proj(2 * SB_WIDTH, SB_WIDTH).astype(BF16)
    gates_ref[:, 0:SB_WIDTH] = _silu(proj(3 * SB_WIDTH, SB_WIDTH)).astype(BF16)
    c0 = 4 * SB_WIDTH
    for g, width in enumerate((LRU_WIDTH, XA_WIDTH)):
        pair = proj(c0, 2 * width)
        feats_ref[:, g * LRU_WIDTH:g * LRU_WIDTH + width] = pair[:, :width]
        gates_ref[:, SB_WIDTH + g * LRU_WIDTH:SB_WIDTH + g * LRU_WIDTH + width] = (
            _silu(pair[:, width:]).astype(BF16))
        c0 += 2 * width


def _in_proj(x2d, norm_g, w_in, layer, qg, kg, ones):
    m = x2d.shape[0]
    row = lambda i: (i, 0)
    const = lambda i: (0, 0)
    return pl.pallas_call(
        _in_proj_kernel,
        grid=(m // ROW_TILE,),
        in_specs=[
            pl.BlockSpec((ROW_TILE, D_MODEL), row),
            pl.BlockSpec((1, D_MODEL), const),
            pl.BlockSpec((None, D_MODEL, D_IN), lambda i: (layer, 0, 0),
                         pipeline_mode=pl.Buffered(1)),
            pl.BlockSpec((1, SB_WIDTH), const),
            pl.BlockSpec((1, SB_WIDTH), const),
            pl.BlockSpec((MXU_WIDTH, MXU_WIDTH), const),
        ],
        out_specs=[
            pl.BlockSpec((ROW_TILE, SB_WIDTH), row),
            pl.BlockSpec((ROW_TILE, SB_WIDTH), row),
            pl.BlockSpec((ROW_TILE, SB_WIDTH), row),
            pl.BlockSpec((ROW_TILE, D_MIX), row),
            pl.BlockSpec((ROW_TILE, LRU_WIDTH + XA_WIDTH), row),
        ],
        out_shape=[
            jax.ShapeDtypeStruct((m, SB_WIDTH), BF16),
            jax.ShapeDtypeStruct((m, SB_WIDTH), BF16),
            jax.ShapeDtypeStruct((m, SB_WIDTH), BF16),
            jax.ShapeDtypeStruct((m, D_MIX), BF16),
            jax.ShapeDtypeStruct((m, LRU_WIDTH + XA_WIDTH), F32),
        ],
        scratch_shapes=[pltpu.VMEM((D_MODEL, D_IN), BF16)],
        compiler_params=pltpu.CompilerParams(
            dimension_semantics=("arbitrary",), vmem_limit_bytes=VMEM_LIMIT),
        name="in_proj",
    )(x2d, norm_g, w_in, qg, kg, ones)


def _log2_terms(z):
    t = jnp.log2(1.0 + jnp.exp2(-jnp.abs(z)))
    log_sig = jnp.minimum(z, 0.0) - t
    return log_sig, log_sig - z


def _sb_attn_kernel(q_ref, k_ref, v_ref, upper_ref, o_ref, *scratch):
    seq = q_ref.shape[0]
    n_blocks = seq // ATT_TILE
    assert ATT_UNROLL % 2 == 0 and n_blocks >= 3
    heads = range(LANES // HEAD_DIM)
    sets = (scratch[0:3], scratch[3:6])
    carry_ref, cmax_ref = scratch[6:8]
    lane = lax.broadcasted_iota(jnp.int32, (1, LANES), 1)
    causal = (lax.broadcasted_iota(jnp.int32, (ATT_QUAD, ATT_QUAD), 1)
              < lax.broadcasted_iota(jnp.int32, (ATT_QUAD, ATT_QUAD), 0))

    def tile_start(j):
        if isinstance(j, int):
            return j * ATT_TILE
        return pl.multiple_of(j * ATT_TILE, ATT_TILE)

    def masked_queries(i):
        q = q_ref[pl.ds(tile_start(i), ATT_TILE), :]
        return [jnp.where((lane >= h * HEAD_DIM) & (lane < (h + 1) * HEAD_DIM), q,
                          jnp.zeros_like(q)) for h in heads]

    def window_start(i):
        return tile_start(max(i - 1, 0) if isinstance(i, int) else jnp.maximum(i - 1, 0))

    def block_rows(i):
        return pl.ds(tile_start(i), ATT_TILE)

    def put(bufs, h, tile, r0, c0, log_sig, log_fail):
        lf_ref, ls_ref, _ = bufs
        nr, nc = log_sig.shape
        lf_row = (2 * h + tile) * ATT_TILE + r0
        lf_ref[lf_row:lf_row + nr, c0:c0 + nc] = log_fail.astype(BF16)
        ls_ref[h * ATT_TILE + r0:h * ATT_TILE + r0 + nr,
               tile * ATT_TILE + c0:tile * ATT_TILE + c0 + nc] = log_sig
        return jnp.sum(log_fail, axis=-1, keepdims=True)

    def put_masked(bufs, h, tile, r0, c0, nr, nc):
        put(bufs, h, tile, r0, c0, jnp.full((nr, nc), MASKED_LOG2, F32), jnp.zeros((nr, nc), F32))

    def full_tile(bufs, h, tile, z):
        return put(bufs, h, tile, 0, 0, *_log2_terms(z))

    def diagonal_tile(bufs, h, tile, z):
        def masked(log_sig, log_fail):
            return jnp.where(causal, log_sig, MASKED_LOG2), jnp.where(causal, log_fail, 0.0)

        top = put(bufs, h, tile, 0, 0, *masked(*_log2_terms(z[:ATT_QUAD, :ATT_QUAD])))
        bottom = put(bufs, h, tile, ATT_QUAD, 0, *_log2_terms(z[ATT_QUAD:, :ATT_QUAD]))
        bottom = bottom + put(bufs, h, tile, ATT_QUAD, ATT_QUAD,
                              *masked(*_log2_terms(z[ATT_QUAD:, ATT_QUAD:])))
        return jnp.concatenate([top, bottom], axis=0)

    def scores(i, bufs, first=False):
        rs_ref = bufs[2]
        kk = k_ref[pl.ds(window_start(i), ATT_WINDOW), :]
        z = _dot_nt(jnp.concatenate(masked_queries(i), axis=0), kk)
        cmax = None
        for h in heads:
            zh = z[h * ATT_TILE:(h + 1) * ATT_TILE]
            if first:
                put_masked(bufs, h, 0, 0, ATT_QUAD, ATT_QUAD, ATT_QUAD)
                rs_old = diagonal_tile(bufs, h, 0, zh[:, :ATT_TILE])
                put_masked(bufs, h, 1, 0, 0, ATT_TILE, ATT_TILE)
                rs_new = jnp.zeros((ATT_TILE, 1), F32)
            else:
                rs_old = full_tile(bufs, h, 0, zh[:, :ATT_TILE])
                rs_new = diagonal_tile(bufs, h, 1, zh[:, ATT_TILE:])
            rs_ref[h] = rs_new
            carry = rs_old + rs_new
            carry_ref[i, h] = carry
            m = jnp.max(carry)
            cmax = m if cmax is None else jnp.maximum(cmax, m)
        cmax_ref[i] = cmax

    def weights(i, bufs):
        lf_ref, ls_ref, rs_ref = bufs
        vv = v_ref[pl.ds(window_start(i), ATT_WINDOW), :]
        later = _dot(lf_ref[...], upper_ref[...])
        ws = []
        for h in heads:
            q_rows = slice(h * ATT_TILE, (h + 1) * ATT_TILE)
            old = slice(2 * h * ATT_TILE, (2 * h + 1) * ATT_TILE)
            new = slice((2 * h + 1) * ATT_TILE, (2 * h + 2) * ATT_TILE)
            x_old = ls_ref[q_rows, 0:ATT_TILE] + (later[old] + rs_ref[h])
            x_new = ls_ref[q_rows, ATT_TILE:ATT_WINDOW] + later[new]
            ws.append(jnp.concatenate([jnp.exp2(x_old), jnp.exp2(x_new)], axis=1))
        pv = _dot(jnp.concatenate(ws, axis=0).astype(BF16), vv)
        o_ref[block_rows(i), :] = jnp.where(lane < HEAD_DIM, pv[:ATT_TILE], pv[ATT_TILE:])

    def finish(i, _):
        qms = masked_queries(i)

        def more(state):
            j, _, cmax = state
            return jnp.logical_and(j >= 0, cmax > F32_EXP2_ZERO)

        def sweep(state):
            j, carries, _ = state
            c0 = pl.multiple_of(j * ATT_TILE, ATT_TILE)
            kt = k_ref[pl.ds(c0, ATT_TILE), :]
            vt = v_ref[pl.ds(c0, ATT_TILE), :]
            new, pvs = [], []
            for h in heads:
                log_sig, log_fail = _log2_terms(_dot_nt(qms[h], kt))
                later = _dot(log_fail.astype(BF16), upper_ref[...])
                w = jnp.exp2(log_sig + later + carries[h])
                pvs.append(_dot(w.astype(BF16), vt))
                new.append(carries[h] + jnp.sum(log_fail, axis=-1, keepdims=True))
            o_ref[block_rows(i), :] += jnp.where(lane < HEAD_DIM, pvs[0], pvs[1])
            return j - 1, tuple(new), jnp.max(jnp.maximum(new[0], new[1]))

        carries = tuple(carry_ref[i, h] for h in heads)
        lax.while_loop(more, sweep, (jnp.maximum(i - 1, 0) - 1, carries, cmax_ref[i]))
        return 0

    def step(i, bufs, other_bufs):
        scores(i, bufs)
        weights(i - 1, other_bufs)

    def unrolled_steps(p, _):
        for k in range(ATT_UNROLL):
            step(ATT_UNROLL * p + 1 + k, sets[(1 + k) % 2], sets[k % 2])
        return 0

    for h in heads:
        put_masked(sets[1], h, 1, 0, ATT_QUAD, ATT_QUAD, ATT_QUAD)
    scores(0, sets[0], first=True)
    n_main = (n_blocks - 1) // ATT_UNROLL
    lax.fori_loop(0, n_main, unrolled_steps, 0)
    for i in range(ATT_UNROLL * n_main + 1, n_blocks):
        step(i, sets[i % 2], sets[(i - 1) % 2])
    weights(n_blocks - 1, sets[(n_blocks - 1) % 2])

    worst = cmax_ref[2]
    for i in range(3, n_blocks):
        worst = jnp.maximum(worst, cmax_ref[i])

    @pl.when(worst > F32_EXP2_ZERO)
    def _():
        lax.fori_loop(2, n_blocks, finish, 0)


def _sb_attn(q, k, v, upper):
    b, s, _ = q.shape
    n_heads = LANES // HEAD_DIM
    blk = pl.BlockSpec((None, s, LANES), lambda bi, hp: (bi, 0, hp))
    return pl.pallas_call(
        _sb_attn_kernel,
        grid=(b, SB_WIDTH // LANES),
        in_specs=[blk, blk, blk,
                  pl.BlockSpec((ATT_TILE, ATT_TILE), lambda bi, hp: (0, 0))],
        out_specs=blk,
        out_shape=jax.ShapeDtypeStruct((b, s, SB_WIDTH), F32),
        scratch_shapes=2 * [
            pltpu.VMEM((n_heads * ATT_WINDOW, ATT_TILE), BF16),
            pltpu.VMEM((n_heads * ATT_TILE, ATT_WINDOW), F32),
            pltpu.VMEM((n_heads, ATT_TILE, 1), F32),
        ] + [
            pltpu.VMEM((s // ATT_TILE, n_heads, ATT_TILE, 1), F32),
            pltpu.SMEM((s // ATT_TILE,), F32),
        ],
        compiler_params=pltpu.CompilerParams(
            dimension_semantics=("arbitrary", "arbitrary"), vmem_limit_bytes=VMEM_LIMIT),
        name="sb_attn",
    )(q, k, v, upper)


def _mem_kv_kernel(mem_ref, g_ref, w_ref, kg_ref, ones_ref, mk_ref, mv_ref):
    x = mem_ref[...]
    ms = jnp.mean(x * x, axis=-1, keepdims=True)
    h = (x * lax.rsqrt(ms + EPS) * g_ref[...]).astype(BF16)
    kv = _dot(h, w_ref[...].astype(BF16))
    mk = _head_rms(kv[:, :XA_WIDTH], ones_ref[...]) * kg_ref[...]
    mk_ref[...] = mk.astype(BF16)
    mv_ref[...] = kv[:, XA_WIDTH:].astype(BF16)


def _mem_kv(mem, mem_g, w_mem_kv, layer, kg, ones):
    b = mem.shape[0]
    const = lambda bi: (0, 0)
    per_b = lambda bi: (bi, 0, 0)
    return pl.pallas_call(
        _mem_kv_kernel,
        grid=(b,),
        in_specs=[
            pl.BlockSpec((None, N_MEM, D_MODEL), per_b),
            pl.BlockSpec((1, D_MODEL), const),
            pl.BlockSpec((None, D_MODEL, 2 * XA_WIDTH), lambda bi: (layer, 0, 0)),
            pl.BlockSpec((1, XA_WIDTH), const),
            pl.BlockSpec((MXU_WIDTH, MXU_WIDTH), const),
        ],
        out_specs=[pl.BlockSpec((None, N_MEM, XA_WIDTH), per_b),
                   pl.BlockSpec((None, N_MEM, XA_WIDTH), per_b)],
        out_shape=[jax.ShapeDtypeStruct((b, N_MEM, XA_WIDTH), BF16),
                   jax.ShapeDtypeStruct((b, N_MEM, XA_WIDTH), BF16)],
        compiler_params=pltpu.CompilerParams(
            dimension_semantics=("arbitrary",), vmem_limit_bytes=VMEM_LIMIT),
        name="mem_kv",
    )(mem, mem_g, w_mem_kv, kg, ones)


def _mix_out_kernel(x_ref, gates_ref, feats_ref, sb_ref, mk_ref, mv_ref, cw_ref, cb_ref,
                    wg_ref, bg_ref, lam_ref, xq_g_ref, ones_ref, wo32_ref, o_ref, ext_ref,
                    h_ref, wo_ref):
    tm = x_ref.shape[0]
    _cast_weight_once(wo32_ref, wo_ref, (pl.program_id(0) == 0) & (pl.program_id(1) == 0))

    @pl.when(pl.program_id(1) == 0)
    def _():
        ext_ref[0:SUBLANES, :] = jnp.zeros((SUBLANES, LRU_WIDTH), F32)
        h_ref[...] = jnp.zeros_like(h_ref)

    o_ref[...] = x_ref[...] + _dot(sb_ref[...].astype(BF16) * gates_ref[:, 0:SB_WIDTH],
                                   wo_ref[0:SB_WIDTH, :])
    lru_x = feats_ref[:, 0:LRU_WIDTH]
    xa_q = feats_ref[:, LRU_WIDTH:]

    ext_ref[SUBLANES:SUBLANES + tm, :] = lru_x
    xc = cb_ref[...] + cw_ref[CONV_WIDTH - 1:CONV_WIDTH, :] * lru_x
    for tap in range(CONV_WIDTH - 1):
        shift = CONV_WIDTH - 1 - tap
        xc = xc + cw_ref[tap:tap + 1, :] * ext_ref[SUBLANES - shift:SUBLANES - shift + tm, :]
    ext_ref[0:SUBLANES, :] = ext_ref[tm:tm + SUBLANES, :]

    gates = _dot(xc.astype(BF16), wg_ref[...]) + bg_ref[...]
    r = _sigmoid(gates[:, :LRU_WIDTH])
    i_gate = _sigmoid(gates[:, LRU_WIDTH:])
    neg_lam = -lam_ref[...]
    softplus_neg_lam = jnp.maximum(neg_lam, 0.0) + jnp.log(1.0 + jnp.exp(-jnp.abs(neg_lam)))
    log_a = (-LRU_C) * r * softplus_neg_lam
    a = jnp.exp(log_a)
    one_minus_a2 = 1.0 - jnp.exp(2.0 * log_a)
    root = jnp.where(one_minus_a2 > 0.0, one_minus_a2 * lax.rsqrt(one_minus_a2), 0.0)
    u = root * (i_gate * xc)
    n_groups = tm // SUBLANES
    a = a.reshape(n_groups, SUBLANES, LRU_WIDTH)
    u = u.reshape(n_groups, SUBLANES, LRU_WIDTH)
    sub = lax.broadcasted_iota(jnp.int32, (1, SUBLANES, LRU_WIDTH), 1)
    d = 1
    while d < SUBLANES:
        keep = sub >= d
        a_prev = jnp.where(keep, pltpu.roll(a, d, 1), 1.0)
        u_prev = jnp.where(keep, pltpu.roll(u, d, 1), 0.0)
        u = u + a * u_prev
        a = a * a_prev
        d *= 2
    h_in = h_ref[...]
    h_before = []
    for g in range(n_groups):
        h_before.append(h_in)
        a_tot = jnp.broadcast_to(a[g, SUBLANES - 1:SUBLANES, :], (SUBLANES, LRU_WIDTH))
        u_tot = jnp.broadcast_to(u[g, SUBLANES - 1:SUBLANES, :], (SUBLANES, LRU_WIDTH))
        h_in = a_tot * h_in + u_tot
    h_ref[...] = h_in
    h = (u + a * jnp.stack(h_before, axis=0)).reshape(tm, LRU_WIDTH)

    lane = lax.broadcasted_iota(jnp.int32, (1, XA_WIDTH), 1)
    qn = _head_rms(xa_q, ones_ref[...]) * xq_g_ref[...]
    mk = mk_ref[...]
    mv = mv_ref[...]
    xa = jnp.zeros((tm, XA_WIDTH), F32)
    for head in range(XA_WIDTH // HEAD_DIM):
        in_head = (lane >= head * HEAD_DIM) & (lane < (head + 1) * HEAD_DIM)
        qh = jnp.where(in_head, qn, 0.0).astype(BF16)
        s = _dot_nt(qh, mk)
        p = jnp.exp2(s - jnp.max(s, axis=-1, keepdims=True))
        denom = jnp.sum(p, axis=-1, keepdims=True)
        oh = _dot(p.astype(BF16), mv)
        xa = jnp.where(in_head, oh / denom, xa)

    y = jnp.concatenate([h, xa], axis=1).astype(BF16) * gates_ref[:, SB_WIDTH:]
    o_ref[...] += _dot(y, wo_ref[SB_WIDTH:, :])


def _mix_out(x, gates, feats, sb, mk, mv, conv_w, conv_b, w_gates, b_gates, lam, xq_g, ones,
             w_out, layer):
    b, s, _ = x.shape
    tile = lambda w: pl.BlockSpec((None, ROW_TILE, w), lambda bi, si: (bi, si, 0))
    per_b = lambda r, w: pl.BlockSpec((None, r, w), lambda bi, si: (bi, 0, 0))
    const = lambda r, w: pl.BlockSpec((r, w), lambda bi, si: (0, 0))
    return pl.pallas_call(
        _mix_out_kernel,
        grid=(b, s // ROW_TILE),
        in_specs=[
            tile(D_MODEL), tile(D_MIX), tile(LRU_WIDTH + XA_WIDTH), tile(SB_WIDTH),
            per_b(N_MEM, XA_WIDTH), per_b(N_MEM, XA_WIDTH),
            const(CONV_WIDTH, LRU_WIDTH), const(1, LRU_WIDTH),
            const(LRU_WIDTH, 2 * LRU_WIDTH), const(1, 2 * LRU_WIDTH),
            const(1, LRU_WIDTH), const(1, XA_WIDTH),
            const(MXU_WIDTH, MXU_WIDTH),
            pl.BlockSpec((None, D_MIX, D_MODEL), lambda bi, si: (layer, 0, 0),
                         pipeline_mode=pl.Buffered(1)),
        ],
        out_specs=tile(D_MODEL),
        out_shape=jax.ShapeDtypeStruct((b, s, D_MODEL), F32),
        scratch_shapes=[pltpu.VMEM((ROW_TILE + SUBLANES, LRU_WIDTH), F32),
                        pltpu.VMEM((SUBLANES, LRU_WIDTH), F32),
                        pltpu.VMEM((D_MIX, D_MODEL), BF16)],
        compiler_params=pltpu.CompilerParams(
            dimension_semantics=("arbitrary", "arbitrary"), vmem_limit_bytes=VMEM_LIMIT),
        name="mix_out",
    )(x, gates, feats, sb, mk, mv, conv_w, conv_b, w_gates, b_gates, lam, xq_g, ones, w_out)


def _block_diag(w):
    n, d, _ = w.shape
    eye = jnp.eye(n, dtype=w.dtype)
    return (eye[:, None, :, None] * w[:, :, None, :]).reshape(n * d, n * d)


def kernel(x, mem, norm_g, w_in, sb_q_g, sb_k_g, conv_w, conv_b, w_rg, b_rg, w_ig, b_ig,
           lru_lambda, xa_q_g, xa_k_g, mem_g, w_mem_kv, w_out):
    b, s, d = x.shape
    depth = norm_g.shape[0]
    scale = HEAD_DIM ** -0.5
    idx = jnp.arange(MXU_WIDTH)
    group_ones = (idx[:, None] // HEAD_DIM == idx[None, :] // HEAD_DIM).astype(BF16)
    upper = (idx[:, None] > idx[None, :]).astype(BF16)

    for l in range(depth):
        qg = (jnp.tile(sb_q_g[l], SB_WIDTH // HEAD_DIM) * (scale * LOG2_E)).reshape(1, SB_WIDTH)
        kg = jnp.tile(sb_k_g[l], SB_WIDTH // HEAD_DIM).reshape(1, SB_WIDTH)
        q, k, v, gates, feats = _in_proj(x.reshape(b * s, d), norm_g[l].reshape(1, d),
                                         w_in, l, qg, kg, group_ones)
        sb = _sb_attn(q.reshape(b, s, SB_WIDTH), k.reshape(b, s, SB_WIDTH),
                      v.reshape(b, s, SB_WIDTH), upper)
        mkg = jnp.tile(xa_k_g[l], XA_WIDTH // HEAD_DIM).reshape(1, XA_WIDTH)
        mk, mv = _mem_kv(mem, mem_g[l].reshape(1, d), w_mem_kv, l, mkg, group_ones)
        w_gates = jnp.concatenate([_block_diag(w_rg[l]), _block_diag(w_ig[l])],
                                  axis=1).astype(BF16)
        b_gates = jnp.concatenate([b_rg[l], b_ig[l]]).reshape(1, 2 * LRU_WIDTH)
        xq_g = (jnp.tile(xa_q_g[l], XA_WIDTH // HEAD_DIM) * (scale * LOG2_E)).reshape(1, XA_WIDTH)
        x = _mix_out(x, gates.reshape(b, s, D_MIX), feats.reshape(b, s, LRU_WIDTH + XA_WIDTH),
                     sb, mk, mv, conv_w[l], conv_b[l].reshape(1, LRU_WIDTH), w_gates, b_gates,
                     lru_lambda[l].reshape(1, LRU_WIDTH), xq_g, group_ones, w_out, l)
    return x
```

```python
import math

import jax
import jax.numpy as jnp
from jax import lax
from jax.experimental import pallas as pl
from jax.experimental.pallas import tpu as pltpu

D_MODEL = 1024
HEAD_DIM = 64
SB_WIDTH = 512
LRU_WIDTH = 256
XA_WIDTH = 256
N_MEM = 256
CONV_WIDTH = 4
LRU_C = 8.0
EPS = 1e-6
D_IN = 4 * SB_WIDTH + 2 * LRU_WIDTH + 2 * XA_WIDTH
D_MIX = SB_WIDTH + LRU_WIDTH + XA_WIDTH

MXU_WIDTH = 256
LANES = 128
SUBLANES = 8
ROW_TILE = 1024
ATT_TILE = MXU_WIDTH
ATT_QUAD = ATT_TILE // 2
ATT_WINDOW = 2 * ATT_TILE
VMEM_LIMIT = 48 * 1024 * 1024

LOG2_E = math.log2(math.e)
F32_EXP2_ZERO = -150.0
MASKED_LOG2 = -1e30

F32 = jnp.float32
BF16 = jnp.bfloat16


def _dot(a, b):
    return jnp.dot(a, b, preferred_element_type=F32)


def _dot_nt(a, b):
    return lax.dot_general(a, b, (((1,), (1,)), ((), ())), preferred_element_type=F32)


def _head_rms(t, group_ones):
    outs = []
    for j in range(t.shape[1] // MXU_WIDTH):
        tj = t[:, j * MXU_WIDTH:(j + 1) * MXU_WIDTH]
        ss = _dot((tj * tj).astype(BF16), group_ones)
        outs.append(tj * lax.rsqrt(ss * (1.0 / HEAD_DIM) + EPS))
    return outs[0] if len(outs) == 1 else jnp.concatenate(outs, axis=1)


def _sigmoid(x):
    return 0.5 * jnp.tanh(0.5 * x) + 0.5


def _silu(x):
    half = 0.5 * x
    return half * jnp.tanh(half) + half


def _cast_weight_once(w_ref, wb_ref, first_step):
    @pl.when(first_step)
    def _():
        for c in range(0, w_ref.shape[1], 2 * MXU_WIDTH):
            wb_ref[:, c:c + 2 * MXU_WIDTH] = w_ref[:, c:c + 2 * MXU_WIDTH].astype(BF16)


def _in_proj_kernel(x_ref, g_ref, w_ref, qg_ref, kg_ref, ones_ref,
                    q_ref, k_ref, v_ref, gates_ref, feats_ref, wb_ref):
    _cast_weight_once(w_ref, wb_ref, pl.program_id(0) == 0)
    x = x_ref[...]
    ms = jnp.mean(x * x, axis=-1, keepdims=True)
    h = (x * lax.rsqrt(ms + EPS) * g_ref[...]).astype(BF16)
    ones = ones_ref[...]

    def proj(c0, width):
        return _dot(h, wb_ref[:, c0:c0 + width])

    q_ref[...] = (_head_rms(proj(0, SB_WIDTH), ones) * qg_ref[...]).astype(BF16)
    k_ref[...] = (_head_rms(proj(SB_WIDTH, SB_WIDTH), ones) * kg_ref[...]).astype(BF16)
    v_ref[...] = proj(2 * SB_WIDTH, SB_WIDTH).astype(BF16)
    gates_ref[:, 0:SB_WIDTH] = _silu(proj(3 * SB_WIDTH, SB_WIDTH)).astype(BF16)
    c0 = 4 * SB_WIDTH
    for g, width in enumerate((LRU_WIDTH, XA_WIDTH)):
        pair = proj(c0, 2 * width)
        feats_ref[:, g * LRU_WIDTH:g * LRU_WIDTH + width] = pair[:, :width]
        gates_ref[:, SB_WIDTH + g * LRU_WIDTH:SB_WIDTH + g * LRU_WIDTH + width] = (
            _silu(pair[:, width:]).astype(BF16))
        c0 += 2 * width


def _in_proj(x2d, norm_g, w_in, layer, qg, kg, ones):
    m = x2d.shape[0]
    row = lambda i: (i, 0)
    const = lambda i: (0, 0)
    return pl.pallas_call(
        _in_proj_kernel,
        grid=(m // ROW_TILE,),
        in_specs=[
            pl.BlockSpec((ROW_TILE, D_MODEL), row),
            pl.BlockSpec((1, D_MODEL), const),
            pl.BlockSpec((None, D_MODEL, D_IN), lambda i: (layer, 0, 0),
                         pipeline_mode=pl.Buffered(1)),
            pl.BlockSpec((1, SB_WIDTH), const),
            pl.BlockSpec((1, SB_WIDTH), const),
            pl.BlockSpec((MXU_WIDTH, MXU_WIDTH), const),
        ],
        out_specs=[
            pl.BlockSpec((ROW_TILE, SB_WIDTH), row),
            pl.BlockSpec((ROW_TILE, SB_WIDTH), row),
            pl.BlockSpec((ROW_TILE, SB_WIDTH), row),
            pl.BlockSpec((ROW_TILE, D_MIX), row),
            pl.BlockSpec((ROW_TILE, LRU_WIDTH + XA_WIDTH), row),
        ],
        out_shape=[
            jax.ShapeDtypeStruct((m, SB_WIDTH), BF16),
            jax.ShapeDtypeStruct((m, SB_WIDTH), BF16),
            jax.ShapeDtypeStruct((m, SB_WIDTH), BF16),
            jax.ShapeDtypeStruct((m, D_MIX), BF16),
            jax.ShapeDtypeStruct((m, LRU_WIDTH + XA_WIDTH), F32),
        ],
        scratch_shapes=[pltpu.VMEM((D_MODEL, D_IN), BF16)],
        compiler_params=pltpu.CompilerParams(
            dimension_semantics=("arbitrary",), vmem_limit_bytes=VMEM_LIMIT),
        name="in_proj",
    )(x2d, norm_g, w_in, qg, kg, ones)


def _log2_terms(z):
    t = jnp.log2(1.0 + jnp.exp2(-jnp.abs(z)))
    log_sig = jnp.minimum(z, 0.0) - t
    return log_sig, log_sig - z


def _sb_attn_kernel(q_ref, k_ref, v_ref, upper_ref, o_ref, *scratch):
    seq = q_ref.shape[0]
    n_blocks = seq // ATT_TILE
    assert n_blocks >= 3
    heads = range(LANES // HEAD_DIM)
    sets = (scratch[0:3], scratch[3:6])
    carry_ref, cmax_ref = scratch[6:8]
    lane = lax.broadcasted_iota(jnp.int32, (1, LANES), 1)
    causal = (lax.broadcasted_iota(jnp.int32, (ATT_QUAD, ATT_QUAD), 1)
              < lax.broadcasted_iota(jnp.int32, (ATT_QUAD, ATT_QUAD), 0))

    def tile_start(j):
        if isinstance(j, int):
            return j * ATT_TILE
        return pl.multiple_of(j * ATT_TILE, ATT_TILE)

    def masked_queries(i):
        q = q_ref[pl.ds(tile_start(i), ATT_TILE), :]
        return [jnp.where((lane >= h * HEAD_DIM) & (lane < (h + 1) * HEAD_DIM), q,
                          jnp.zeros_like(q)) for h in heads]

    def window_start(i):
        return tile_start(max(i - 1, 0) if isinstance(i, int) else jnp.maximum(i - 1, 0))

    def block_rows(i):
        return pl.ds(tile_start(i), ATT_TILE)

    def put(bufs, h, tile, r0, c0, log_sig, log_fail):
        lf_ref, ls_ref, _ = bufs
        nr, nc = log_sig.shape
        lf_row = (2 * h + tile) * ATT_TILE + r0
        lf_ref[lf_row:lf_row + nr, c0:c0 + nc] = log_fail.astype(BF16)
        ls_ref[h * ATT_TILE + r0:h * ATT_TILE + r0 + nr,
               tile * ATT_TILE + c0:tile * ATT_TILE + c0 + nc] = log_sig
        return jnp.sum(log_fail, axis=-1, keepdims=True)

    def put_masked(bufs, h, tile, r0, c0, nr, nc):
        put(bufs, h, tile, r0, c0, jnp.full((nr, nc), MASKED_LOG2, F32), jnp.zeros((nr, nc), F32))

    def full_tile(bufs, h, tile, z):
        return put(bufs, h, tile, 0, 0, *_log2_terms(z))

    def diagonal_tile(bufs, h, tile, z):
        def masked(log_sig, log_fail):
            return jnp.where(causal, log_sig, MASKED_LOG2), jnp.where(causal, log_fail, 0.0)

        top = put(bufs, h, tile, 0, 0, *masked(*_log2_terms(z[:ATT_QUAD, :ATT_QUAD])))
        bottom = put(bufs, h, tile, ATT_QUAD, 0, *_log2_terms(z[ATT_QUAD:, :ATT_QUAD]))
        bottom = bottom + put(bufs, h, tile, ATT_QUAD, ATT_QUAD,
                              *masked(*_log2_terms(z[ATT_QUAD:, ATT_QUAD:])))
        return jnp.concatenate([top, bottom], axis=0)

    def scores(i, bufs, first=False):
        rs_ref = bufs[2]
        kk = k_ref[pl.ds(window_start(i), ATT_WINDOW), :]
        z = _dot_nt(jnp.concatenate(masked_queries(i), axis=0), kk)
        cmax = None
        for h in heads:
            zh = z[h * ATT_TILE:(h + 1) * ATT_TILE]
            if first:
                put_masked(bufs, h, 0, 0, ATT_QUAD, ATT_QUAD, ATT_QUAD)
                rs_old = diagonal_tile(bufs, h, 0, zh[:, :ATT_TILE])
                put_masked(bufs, h, 1, 0, 0, ATT_TILE, ATT_TILE)
                rs_new = jnp.zeros((ATT_TILE, 1), F32)
            else:
                rs_old = full_tile(bufs, h, 0, zh[:, :ATT_TILE])
                rs_new = diagonal_tile(bufs, h, 1, zh[:, ATT_TILE:])
            rs_ref[h] = rs_new
            carry = rs_old + rs_new
            carry_ref[i, h] = carry
            m = jnp.max(carry)
            cmax = m if cmax is None else jnp.maximum(cmax, m)
        cmax_ref[i] = cmax

    def weights(i, bufs):
        lf_ref, ls_ref, rs_ref = bufs
        vv = v_ref[pl.ds(window_start(i), ATT_WINDOW), :]
        later = _dot(lf_ref[...], upper_ref[...])
        ws = []
        for h in heads:
            q_rows = slice(h * ATT_TILE, (h + 1) * ATT_TILE)
            old = slice(2 * h * ATT_TILE, (2 * h + 1) * ATT_TILE)
            new = slice((2 * h + 1) * ATT_TILE, (2 * h + 2) * ATT_TILE)
            x_old = ls_ref[q_rows, 0:ATT_TILE] + (later[old] + rs_ref[h])
            x_new = ls_ref[q_rows, ATT_TILE:ATT_WINDOW] + later[new]
            ws.append(jnp.concatenate([jnp.exp2(x_old), jnp.exp2(x_new)], axis=1))
        pv = _dot(jnp.concatenate(ws, axis=0).astype(BF16), vv)
        o_ref[block_rows(i), :] = jnp.where(lane < HEAD_DIM, pv[:ATT_TILE], pv[ATT_TILE:])

    def finish(i, _):
        qms = masked_queries(i)

        def more(state):
            j, _, cmax = state
            return jnp.logical_and(j >= 0, cmax > F32_EXP2_ZERO)

        def sweep(state):
            j, carries, _ = state
            c0 = pl.multiple_of(j * ATT_TILE, ATT_TILE)
            kt = k_ref[pl.ds(c0, ATT_TILE), :]
            vt = v_ref[pl.ds(c0, ATT_TILE), :]
            new, pvs = [], []
            for h in heads:
                log_sig, log_fail = _log2_terms(_dot_nt(qms[h], kt))
                later = _dot(log_fail.astype(BF16), upper_ref[...])
                w = jnp.exp2(log_sig + later + carries[h])
                pvs.append(_dot(w.astype(BF16), vt))
                new.append(carries[h] + jnp.sum(log_fail, axis=-1, keepdims=True))
            o_ref[block_rows(i), :] += jnp.where(lane < HEAD_DIM, pvs[0], pvs[1])
            return j - 1, tuple(new), jnp.max(jnp.maximum(new[0], new[1]))

        carries = tuple(carry_ref[i, h] for h in heads)
        lax.while_loop(more, sweep, (jnp.maximum(i - 1, 0) - 1, carries, cmax_ref[i]))
        return 0

    for h in heads:
        put_masked(sets[1], h, 1, 0, ATT_QUAD, ATT_QUAD, ATT_QUAD)
    scores(0, sets[0], first=True)
    for i in range(1, n_blocks):
        scores(i, sets[i % 2])
        weights(i - 1, sets[(i - 1) % 2])
    weights(n_blocks - 1, sets[(n_blocks - 1) % 2])

    worst = cmax_ref[2]
    for i in range(3, n_blocks):
        worst = jnp.maximum(worst, cmax_ref[i])

    @pl.when(worst > F32_EXP2_ZERO)
    def _():
        lax.fori_loop(2, n_blocks, finish, 0)


def _sb_attn(q, k, v, upper):
    b, s, _ = q.shape
    n_heads = LANES // HEAD_DIM
    blk = pl.BlockSpec((None, s, LANES), lambda bi, hp: (bi, 0, hp))
    return pl.pallas_call(
        _sb_attn_kernel,
        grid=(b, SB_WIDTH // LANES),
        in_specs=[blk, blk, blk,
                  pl.BlockSpec((ATT_TILE, ATT_TILE), lambda bi, hp: (0, 0))],
        out_specs=blk,
        out_shape=jax.ShapeDtypeStruct((b, s, SB_WIDTH), F32),
        scratch_shapes=2 * [
            pltpu.VMEM((n_heads * ATT_WINDOW, ATT_TILE), BF16),
            pltpu.VMEM((n_heads * ATT_TILE, ATT_WINDOW), F32),
            pltpu.VMEM((n_heads, ATT_TILE, 1), F32),
        ] + [
            pltpu.VMEM((s // ATT_TILE, n_heads, ATT_TILE, 1), F32),
            pltpu.SMEM((s // ATT_TILE,), F32),
        ],
        compiler_params=pltpu.CompilerParams(
            dimension_semantics=("arbitrary", "arbitrary"), vmem_limit_bytes=VMEM_LIMIT),
        name="sb_attn",
    )(q, k, v, upper)


def _mem_kv_kernel(mem_ref, g_ref, w_ref, kg_ref, ones_ref, mk_ref, mv_ref):
    x = mem_ref[...]
    ms = jnp.mean(x * x, axis=-1, keepdims=True)
    h = (x * lax.rsqrt(ms + EPS) * g_ref[...]).astype(BF16)
    kv = _dot(h, w_ref[...].astype(BF16))
    mk = _head_rms(kv[:, :XA_WIDTH], ones_ref[...]) * kg_ref[...]
    mk_ref[...] = mk.astype(BF16)
    mv_ref[...] = kv[:, XA_WIDTH:].astype(BF16)


def _mem_kv(mem, mem_g, w_mem_kv, layer, kg, ones):
    b = mem.shape[0]
    const = lambda bi: (0, 0)
    per_b = lambda bi: (bi, 0, 0)
    return pl.pallas_call(
        _mem_kv_kernel,
        grid=(b,),
        in_specs=[
            pl.BlockSpec((None, N_MEM, D_MODEL), per_b),
            pl.BlockSpec((1, D_MODEL), const),
            pl.BlockSpec((None, D_MODEL, 2 * XA_WIDTH), lambda bi: (layer, 0, 0)),
            pl.BlockSpec((1, XA_WIDTH), const),
            pl.BlockSpec((MXU_WIDTH, MXU_WIDTH), const),
        ],
        out_specs=[pl.BlockSpec((None, N_MEM, XA_WIDTH), per_b),
                   pl.BlockSpec((None, N_MEM, XA_WIDTH), per_b)],
        out_shape=[jax.ShapeDtypeStruct((b, N_MEM, XA_WIDTH), BF16),
                   jax.ShapeDtypeStruct((b, N_MEM, XA_WIDTH), BF16)],
        compiler_params=pltpu.CompilerParams(
            dimension_semantics=("arbitrary",), vmem_limit_bytes=VMEM_LIMIT),
        name="mem_kv",
    )(mem, mem_g, w_mem_kv, kg, ones)


def _mix_out_kernel(x_ref, gates_ref, feats_ref, sb_ref, mk_ref, mv_ref, cw_ref, cb_ref,
                    wg_ref, bg_ref, lam_ref, xq_g_ref, ones_ref, wo32_ref, o_ref, ext_ref,
                    h_ref, wo_ref):
    tm = x_ref.shape[0]
    _cast_weight_once(wo32_ref, wo_ref, (pl.program_id(0) == 0) & (pl.program_id(1) == 0))

    @pl.when(pl.program_id(1) == 0)
    def _():
        ext_ref[0:SUBLANES, :] = jnp.zeros((SUBLANES, LRU_WIDTH), F32)
        h_ref[...] = jnp.zeros_like(h_ref)

    o_ref[...] = x_ref[...] + _dot(sb_ref[...].astype(BF16) * gates_ref[:, 0:SB_WIDTH],
                                   wo_ref[0:SB_WIDTH, :])
    lru_x = feats_ref[:, 0:LRU_WIDTH]
    xa_q = feats_ref[:, LRU_WIDTH:]

    ext_ref[SUBLANES:SUBLANES + tm, :] = lru_x
    xc = cb_ref[...] + cw_ref[CONV_WIDTH - 1:CONV_WIDTH, :] * lru_x
    for tap in range(CONV_WIDTH - 1):
        shift = CONV_WIDTH - 1 - tap
        xc = xc + cw_ref[tap:tap + 1, :] * ext_ref[SUBLANES - shift:SUBLANES - shift + tm, :]
    ext_ref[0:SUBLANES, :] = ext_ref[tm:tm + SUBLANES, :]

    gates = _dot(xc.astype(BF16), wg_ref[...]) + bg_ref[...]
    r = _sigmoid(gates[:, :LRU_WIDTH])
    i_gate = _sigmoid(gates[:, LRU_WIDTH:])
    neg_lam = -lam_ref[...]
    softplus_neg_lam = jnp.maximum(neg_lam, 0.0) + jnp.log(1.0 + jnp.exp(-jnp.abs(neg_lam)))
    log_a = (-LRU_C) * r * softplus_neg_lam
    a = jnp.exp(log_a)
    one_minus_a2 = 1.0 - jnp.exp(2.0 * log_a)
    root = jnp.where(one_minus_a2 > 0.0, one_minus_a2 * lax.rsqrt(one_minus_a2), 0.0)
    u = root * (i_gate * xc)
    n_groups = tm // SUBLANES
    a = a.reshape(n_groups, SUBLANES, LRU_WIDTH)
    u = u.reshape(n_groups, SUBLANES, LRU_WIDTH)
    sub = lax.broadcasted_iota(jnp.int32, (1, SUBLANES, LRU_WIDTH), 1)
    d = 1
    while d < SUBLANES:
        keep = sub >= d
        a_prev = jnp.where(keep, pltpu.roll(a, d, 1), 1.0)
        u_prev = jnp.where(keep, pltpu.roll(u, d, 1), 0.0)
        u = u + a * u_prev
        a = a * a_prev
        d *= 2
    h_in = h_ref[...]
    h_before = []
    for g in range(n_groups):
        h_before.append(h_in)
        a_tot = jnp.broadcast_to(a[g, SUBLANES - 1:SUBLANES, :], (SUBLANES, LRU_WIDTH))
        u_tot = jnp.broadcast_to(u[g, SUBLANES - 1:SUBLANES, :], (SUBLANES, LRU_WIDTH))
        h_in = a_tot * h_in + u_tot
    h_ref[...] = h_in
    h = (u + a * jnp.stack(h_before, axis=0)).reshape(tm, LRU_WIDTH)

    lane = lax.broadcasted_iota(jnp.int32, (1, XA_WIDTH), 1)
    qn = _head_rms(xa_q, ones_ref[...]) * xq_g_ref[...]
    mk = mk_ref[...]
    mv = mv_ref[...]
    xa = jnp.zeros((tm, XA_WIDTH), F32)
    for head in range(XA_WIDTH // HEAD_DIM):
        in_head = (lane >= head * HEAD_DIM) & (lane < (head + 1) * HEAD_DIM)
        qh = jnp.where(in_head, qn, 0.0).astype(BF16)
        s = _dot_nt(qh, mk)
        p = jnp.exp2(s - jnp.max(s, axis=-1, keepdims=True))
        denom = jnp.sum(p, axis=-1, keepdims=True)
        oh = _dot(p.astype(BF16), mv)
        xa = jnp.where(in_head, oh / denom, xa)

    y = jnp.concatenate([h, xa], axis=1).astype(BF16) * gates_ref[:, SB_WIDTH:]
    o_ref[...] += _dot(y, wo_ref[SB_WIDTH:, :])


def _mix_out(x, gates, feats, sb, mk, mv, conv_w, conv_b, w_gates, b_gates, lam, xq_g, ones,
             w_out, layer):
    b, s, _ = x.shape
    tile = lambda w: pl.BlockSpec((None, ROW_TILE, w), lambda bi, si: (bi, si, 0))
    per_b = lambda r, w: pl.BlockSpec((None, r, w), lambda bi, si: (bi, 0, 0))
    const = lambda r, w: pl.BlockSpec((r, w), lambda bi, si: (0, 0))
    return pl.pallas_call(
        _mix_out_kernel,
        grid=(b, s // ROW_TILE),
        in_specs=[
            tile(D_MODEL), tile(D_MIX), tile(LRU_WIDTH + XA_WIDTH), tile(SB_WIDTH),
            per_b(N_MEM, XA_WIDTH), per_b(N_MEM, XA_WIDTH),
            const(CONV_WIDTH, LRU_WIDTH), const(1, LRU_WIDTH),
            const(LRU_WIDTH, 2 * LRU_WIDTH), const(1, 2 * LRU_WIDTH),
            const(1, LRU_WIDTH), const(1, XA_WIDTH),
            const(MXU_WIDTH, MXU_WIDTH),
            pl.BlockSpec((None, D_MIX, D_MODEL), lambda bi, si: (layer, 0, 0),
                         pipeline_mode=pl.Buffered(1)),
        ],
        out_specs=tile(D_MODEL),
        out_shape=jax.ShapeDtypeStruct((b, s, D_MODEL), F32),
        scratch_shapes=[pltpu.VMEM((ROW_TILE + SUBLANES, LRU_WIDTH), F32),
                        pltpu.VMEM((SUBLANES, LRU_WIDTH), F32),
                        pltpu.VMEM((D_MIX, D_MODEL), BF16)],
        compiler_params=pltpu.CompilerParams(
            dimension_semantics=("arbitrary", "arbitrary"), vmem_limit_bytes=VMEM_LIMIT),
        name="mix_out",
    )(x, gates, feats, sb, mk, mv, conv_w, conv_b, w_gates, b_gates, lam, xq_g, ones, w_out)


def _block_diag(w):
    n, d, _ = w.shape
    eye = jnp.eye(n, dtype=w.dtype)
    return (eye[:, None, :, None] * w[:, :, None, :]).reshape(n * d, n * d)


def kernel(x, mem, norm_g, w_in, sb_q_g, sb_k_g, conv_w, conv_b, w_rg, b_rg, w_ig, b_ig,
           lru_lambda, xa_q_g, xa_k_g, mem_g, w_mem_kv, w_out):
    b, s, d = x.shape
    depth = norm_g.shape[0]
    scale = HEAD_DIM ** -0.5
    idx = jnp.arange(MXU_WIDTH)
    group_ones = (idx[:, None] // HEAD_DIM == idx[None, :] // HEAD_DIM).astype(BF16)
    upper = (idx[:, None] > idx[None, :]).astype(BF16)

    for l in range(depth):
        qg = (jnp.tile(sb_q_g[l], SB_WIDTH // HEAD_DIM) * (scale * LOG2_E)).reshape(1, SB_WIDTH)
        kg = jnp.tile(sb_k_g[l], SB_WIDTH // HEAD_DIM).reshape(1, SB_WIDTH)
        q, k, v, gates, feats = _in_proj(x.reshape(b * s, d), norm_g[l].reshape(1, d),
                                         w_in, l, qg, kg, group_ones)
        sb = _sb_attn(q.reshape(b, s, SB_WIDTH), k.reshape(b, s, SB_WIDTH),
                      v.reshape(b, s, SB_WIDTH), upper)
        mkg = jnp.tile(xa_k_g[l], XA_WIDTH // HEAD_DIM).reshape(1, XA_WIDTH)
        mk, mv = _mem_kv(mem, mem_g[l].reshape(1, d), w_mem_kv, l, mkg, group_ones)
        w_gates = jnp.concatenate([_block_diag(w_rg[l]), _block_diag(w_ig[l])],
                                  axis=1).astype(BF16)
        b_gates = jnp.concatenate([b_rg[l], b_ig[l]]).reshape(1, 2 * LRU_WIDTH)
        xq_g = (jnp.tile(xa_q_g[l], XA_WIDTH // HEAD_DIM) * (scale * LOG2_E)).reshape(1, XA_WIDTH)
        x = _mix_out(x, gates.reshape(b, s, D_MIX), feats.reshape(b, s, LRU_WIDTH + XA_WIDTH),
                     sb, mk, mv, conv_w[l], conv_b[l].reshape(1, LRU_WIDTH), w_gates, b_gates,
                     lru_lambda[l].reshape(1, LRU_WIDTH), xq_g, group_ones, w_out, l)
    return x
```

```python
import math

import jax
import jax.numpy as jnp
from jax import lax
from jax.experimental import pallas as pl
from jax.experimental.pallas import tpu as pltpu

D_MODEL = 1024
HEAD_DIM = 64
SB_WIDTH = 512
LRU_WIDTH = 256
XA_WIDTH = 256
N_MEM = 256
CONV_WIDTH = 4
LRU_C = 8.0
EPS = 1e-6
D_IN = 4 * SB_WIDTH + 2 * LRU_WIDTH + 2 * XA_WIDTH
D_MIX = SB_WIDTH + LRU_WIDTH + XA_WIDTH

MXU_WIDTH = 256
LANES = 128
SUBLANES = 8
ROW_TILE = 1024
ATT_TILE = MXU_WIDTH
ATT_QUAD = ATT_TILE // 2
ATT_WINDOW = 2 * ATT_TILE
VMEM_LIMIT = 48 * 1024 * 1024

LOG2_E = math.log2(math.e)
F32_EXP2_ZERO = -150.0
MASKED_LOG2 = -1e30

F32 = jnp.float32
BF16 = jnp.bfloat16


def _dot(a, b):
    return jnp.dot(a, b, preferred_element_type=F32)


def _dot_nt(a, b):
    return lax.dot_general(a, b, (((1,), (1,)), ((), ())), preferred_element_type=F32)


def _head_rms(t, group_ones):
    outs = []
    for j in range(t.shape[1] // MXU_WIDTH):
        tj = t[:, j * MXU_WIDTH:(j + 1) * MXU_WIDTH]
        ss = _dot((tj * tj).astype(BF16), group_ones)
        outs.append(tj * lax.rsqrt(ss * (1.0 / HEAD_DIM) + EPS))
    return outs[0] if len(outs) == 1 else jnp.concatenate(outs, axis=1)


def _sigmoid(x):
    return 0.5 * jnp.tanh(0.5 * x) + 0.5


def _silu(x):
    half = 0.5 * x
    return half * jnp.tanh(half) + half


def _cast_weight_once(w_ref, wb_ref, first_step):
    @pl.when(first_step)
    def _():
        for c in range(0, w_ref.shape[1], 2 * MXU_WIDTH):
            wb_ref[:, c:c + 2 * MXU_WIDTH] = w_ref[:, c:c + 2 * MXU_WIDTH].astype(BF16)


def _in_proj_kernel(x_ref, g_ref, w_ref, qg_ref, kg_ref, ones_ref,
                    q_ref, k_ref, v_ref, gates_ref, feats_ref, wb_ref):
    _cast_weight_once(w_ref, wb_ref, pl.program_id(0) == 0)
    x = x_ref[...]
    ms = jnp.mean(x * x, axis=-1, keepdims=True)
    h = (x * lax.rsqrt(ms + EPS) * g_ref[...]).astype(BF16)
    ones = ones_ref[...]

    def proj(c0, width):
        return _dot(h, wb_ref[:, c0:c0 + width])

    q_ref[...] = (_head_rms(proj(0, SB_WIDTH), ones) * qg_ref[...]).astype(BF16)
    k_ref[...] = (_head_rms(proj(SB_WIDTH, SB_WIDTH), ones) * kg_ref[...]).astype(BF16)
    v_ref[...] = proj(2 * SB_WIDTH, SB_WIDTH).astype(BF16)
    gates_ref[:, 0:SB_WIDTH] = _silu(proj(3 * SB_WIDTH, SB_WIDTH)).astype(BF16)
    c0 = 4 * SB_WIDTH
    for g, width in enumerate((LRU_WIDTH, XA_WIDTH)):
        pair = proj(c0, 2 * width)
        feats_ref[:, g * LRU_WIDTH:g * LRU_WIDTH + width] = pair[:, :width]
        gates_ref[:, SB_WIDTH + g * LRU_WIDTH:SB_WIDTH + g * LRU_WIDTH + width] = (
            _silu(pair[:, width:]).astype(BF16))
        c0 += 2 * width


def _in_proj(x2d, norm_g, w_in, layer, qg, kg, ones):
    m = x2d.shape[0]
    row = lambda i: (i, 0)
    const = lambda i: (0, 0)
    return pl.pallas_call(
        _in_proj_kernel,
        grid=(m // ROW_TILE,),
        in_specs=[
            pl.BlockSpec((ROW_TILE, D_MODEL), row),
            pl.BlockSpec((1, D_MODEL), const),
            pl.BlockSpec((None, D_MODEL, D_IN), lambda i: (layer, 0, 0),
                         pipeline_mode=pl.Buffered(1)),
            pl.BlockSpec((1, SB_WIDTH), const),
            pl.BlockSpec((1, SB_WIDTH), const),
            pl.BlockSpec((MXU_WIDTH, MXU_WIDTH), const),
        ],
        out_specs=[
            pl.BlockSpec((ROW_TILE, SB_WIDTH), row),
            pl.BlockSpec((ROW_TILE, SB_WIDTH), row),
            pl.BlockSpec((ROW_TILE, SB_WIDTH), row),
            pl.BlockSpec((ROW_TILE, D_MIX), row),
            pl.BlockSpec((ROW_TILE, LRU_WIDTH + XA_WIDTH), row),
        ],
        out_shape=[
            jax.ShapeDtypeStruct((m, SB_WIDTH), BF16),
            jax.ShapeDtypeStruct((m, SB_WIDTH), BF16),
            jax.ShapeDtypeStruct((m, SB_WIDTH), BF16),
            jax.ShapeDtypeStruct((m, D_MIX), BF16),
            jax.ShapeDtypeStruct((m, LRU_WIDTH + XA_WIDTH), F32),
        ],
        scratch_shapes=[pltpu.VMEM((D_MODEL, D_IN), BF16)],
        compiler_params=pltpu.CompilerParams(
            dimension_semantics=("arbitrary",), vmem_limit_bytes=VMEM_LIMIT),
        name="in_proj",
    )(x2d, norm_g, w_in, qg, kg, ones)


def _log2_terms(z):
    t = jnp.log2(1.0 + jnp.exp2(-jnp.abs(z)))
    log_sig = jnp.minimum(z, 0.0) - t
    return log_sig, log_sig - z


def _sb_attn_kernel(q_ref, k_ref, v_ref, upper_ref, o_ref, *scratch):
    seq = q_ref.shape[0]
    n_blocks = seq // ATT_TILE
    assert n_blocks >= 3
    heads = range(LANES // HEAD_DIM)
    sets = (scratch[0:3], scratch[3:6])
    carry_ref, cmax_ref = scratch[6:8]
    lane = lax.broadcasted_iota(jnp.int32, (1, LANES), 1)
    causal = (lax.broadcasted_iota(jnp.int32, (ATT_QUAD, ATT_QUAD), 1)
              < lax.broadcasted_iota(jnp.int32, (ATT_QUAD, ATT_QUAD), 0))

    def tile_start(j):
        if isinstance(j, int):
            return j * ATT_TILE
        return pl.multiple_of(j * ATT_TILE, ATT_TILE)

    def masked_queries(i):
        q = q_ref[pl.ds(tile_start(i), ATT_TILE), :]
        return [jnp.where((lane >= h * HEAD_DIM) & (lane < (h + 1) * HEAD_DIM), q,
                          jnp.zeros_like(q)) for h in heads]

    def window_start(i):
        return tile_start(max(i - 1, 0) if isinstance(i, int) else jnp.maximum(i - 1, 0))

    def block_rows(i):
        return pl.ds(tile_start(i), ATT_TILE)

    def put(bufs, h, tile, r0, c0, log_sig, log_fail):
        lf_ref, ls_ref, _ = bufs
        nr, nc = log_sig.shape
        lf_row = (2 * h + tile) * ATT_TILE + r0
        lf_ref[lf_row:lf_row + nr, c0:c0 + nc] = log_fail.astype(BF16)
        ls_ref[h * ATT_TILE + r0:h * ATT_TILE + r0 + nr,
               tile * ATT_TILE + c0:tile * ATT_TILE + c0 + nc] = log_sig
        return jnp.sum(log_fail, axis=-1, keepdims=True)

    def put_masked(bufs, h, tile, r0, c0, nr, nc):
        put(bufs, h, tile, r0, c0, jnp.full((nr, nc), MASKED_LOG2, F32), jnp.zeros((nr, nc), F32))

    def full_tile(bufs, h, tile, z):
        return put(bufs, h, tile, 0, 0, *_log2_terms(z))

    def diagonal_tile(bufs, h, tile, z):
        def masked(log_sig, log_fail):
            return jnp.where(causal, log_sig, MASKED_LOG2), jnp.where(causal, log_fail, 0.0)

        top = put(bufs, h, tile, 0, 0, *masked(*_log2_terms(z[:ATT_QUAD, :ATT_QUAD])))
        bottom = put(bufs, h, tile, ATT_QUAD, 0, *_log2_terms(z[ATT_QUAD:, :ATT_QUAD]))
        bottom = bottom + put(bufs, h, tile, ATT_QUAD, ATT_QUAD,
                              *masked(*_log2_terms(z[ATT_QUAD:, ATT_QUAD:])))
        return jnp.concatenate([top, bottom], axis=0)

    def scores(i, bufs, first=False):
        rs_ref = bufs[2]
        kk = k_ref[pl.ds(window_start(i), ATT_WINDOW), :]
        z = _dot_nt(jnp.concatenate(masked_queries(i), axis=0), kk)
        cmax = None
        for h in heads:
            zh = z[h * ATT_TILE:(h + 1) * ATT_TILE]
            if first:
                put_masked(bufs, h, 0, 0, ATT_QUAD, ATT_QUAD, ATT_QUAD)
                rs_old = diagonal_tile(bufs, h, 0, zh[:, :ATT_TILE])
                put_masked(bufs, h, 1, 0, 0, ATT_TILE, ATT_TILE)
                rs_new = jnp.zeros((ATT_TILE, 1), F32)
            else:
                rs_old = full_tile(bufs, h, 0, zh[:, :ATT_TILE])
                rs_new = diagonal_tile(bufs, h, 1, zh[:, ATT_TILE:])
            rs_ref[h] = rs_new
            carry = rs_old + rs_new
            carry_ref[i, h] = carry
            m = jnp.max(carry)
            cmax = m if cmax is None else jnp.maximum(cmax, m)
        cmax_ref[i] = cmax

    def weights(i, bufs):
        lf_ref, ls_ref, rs_ref = bufs
        vv = v_ref[pl.ds(window_start(i), ATT_WINDOW), :]
        later = _dot(lf_ref[...], upper_ref[...])
        ws = []
        for h in heads:
            q_rows = slice(h * ATT_TILE, (h + 1) * ATT_TILE)
            old = slice(2 * h * ATT_TILE, (2 * h + 1) * ATT_TILE)
            new = slice((2 * h + 1) * ATT_TILE, (2 * h + 2) * ATT_TILE)
            x_old = ls_ref[q_rows, 0:ATT_TILE] + (later[old] + rs_ref[h])
            x_new = ls_ref[q_rows, ATT_TILE:ATT_WINDOW] + later[new]
            ws.append(jnp.concatenate([jnp.exp2(x_old), jnp.exp2(x_new)], axis=1))
        pv = _dot(jnp.concatenate(ws, axis=0).astype(BF16), vv)
        o_ref[block_rows(i), :] = jnp.where(lane < HEAD_DIM, pv[:ATT_TILE], pv[ATT_TILE:])

    def finish(i, _):
        qms = masked_queries(i)

        def more(state):
            j, _, cmax = state
            return jnp.logical_and(j >= 0, cmax > F32_EXP2_ZERO)

        def sweep(state):
            j, carries, _ = state
            c0 = pl.multiple_of(j * ATT_TILE, ATT_TILE)
            kt = k_ref[pl.ds(c0, ATT_TILE), :]
            vt = v_ref[pl.ds(c0, ATT_TILE), :]
            new, pvs = [], []
            for h in heads:
                log_sig, log_fail = _log2_terms(_dot_nt(qms[h], kt))
                later = _dot(log_fail.astype(BF16), upper_ref[...])
                w = jnp.exp2(log_sig + later + carries[h])
                pvs.append(_dot(w.astype(BF16), vt))
                new.append(carries[h] + jnp.sum(log_fail, axis=-1, keepdims=True))
            o_ref[block_rows(i), :] += jnp.where(lane < HEAD_DIM, pvs[0], pvs[1])
            return j - 1, tuple(new), jnp.max(jnp.maximum(new[0], new[1]))

        carries = tuple(carry_ref[i, h] for h in heads)
        lax.while_loop(more, sweep, (jnp.maximum(i - 1, 0) - 1, carries, cmax_ref[i]))
        return 0

    for h in heads:
        put_masked(sets[1], h, 1, 0, ATT_QUAD, ATT_QUAD, ATT_QUAD)
    scores(0, sets[0], first=True)
    for i in range(1, n_blocks):
        scores(i, sets[i % 2])
        weights(i - 1, sets[(i - 1) % 2])
    weights(n_blocks - 1, sets[(n_blocks - 1) % 2])

    worst = cmax_ref[2]
    for i in range(3, n_blocks):
        worst = jnp.maximum(worst, cmax_ref[i])

    @pl.when(worst > F32_EXP2_ZERO)
    def _():
        lax.fori_loop(2, n_blocks, finish, 0)


def _sb_attn(q, k, v, upper):
    b, s, _ = q.shape
    n_heads = LANES // HEAD_DIM
    blk = pl.BlockSpec((None, s, LANES), lambda bi, hp: (bi, 0, hp))
    return pl.pallas_call(
        _sb_attn_kernel,
        grid=(b, SB_WIDTH // LANES),
        in_specs=[blk, blk, blk,
                  pl.BlockSpec((ATT_TILE, ATT_TILE), lambda bi, hp: (0, 0))],
        out_specs=blk,
        out_shape=jax.ShapeDtypeStruct((b, s, SB_WIDTH), F32),
        scratch_shapes=2 * [
            pltpu.VMEM((n_heads * ATT_WINDOW, ATT_TILE), BF16),
            pltpu.VMEM((n_heads * ATT_TILE, ATT_WINDOW), F32),
            pltpu.VMEM((n_heads, ATT_TILE, 1), F32),
        ] + [
            pltpu.VMEM((s // ATT_TILE, n_heads, ATT_TILE, 1), F32),
            pltpu.SMEM((s // ATT_TILE,), F32),
        ],
        compiler_params=pltpu.CompilerParams(
            dimension_semantics=("arbitrary", "arbitrary"), vmem_limit_bytes=VMEM_LIMIT),
        name="sb_attn",
    )(q, k, v, upper)


def _mem_kv_kernel(mem_ref, g_ref, w_ref, kg_ref, ones_ref, mk_ref, mv_ref):
    x = mem_ref[...]
    ms = jnp.mean(x * x, axis=-1, keepdims=True)
    h = (x * lax.rsqrt(ms + EPS) * g_ref[...]).astype(BF16)
    kv = _dot(h, w_ref[...].astype(BF16))
    mk = _head_rms(kv[:, :XA_WIDTH], ones_ref[...]) * kg_ref[...]
    mk_ref[...] = mk.astype(BF16)
    mv_ref[...] = kv[:, XA_WIDTH:].astype(BF16)


def _mem_kv(mem, mem_g, w_mem_kv, layer, kg, ones):
    b = mem.shape[0]
    const = lambda bi: (0, 0)
    per_b = lambda bi: (bi, 0, 0)
    return pl.pallas_call(
        _mem_kv_kernel,
        grid=(b,),
        in_specs=[
            pl.BlockSpec((None, N_MEM, D_MODEL), per_b),
            pl.BlockSpec((1, D_MODEL), const),
            pl.BlockSpec((None, D_MODEL, 2 * XA_WIDTH), lambda bi: (layer, 0, 0)),
            pl.BlockSpec((1, XA_WIDTH), const),
            pl.BlockSpec((MXU_WIDTH, MXU_WIDTH), const),
        ],
        out_specs=[pl.BlockSpec((None, N_MEM, XA_WIDTH), per_b),
                   pl.BlockSpec((None, N_MEM, XA_WIDTH), per_b)],
        out_shape=[jax.ShapeDtypeStruct((b, N_MEM, XA_WIDTH), BF16),
                   jax.ShapeDtypeStruct((b, N_MEM, XA_WIDTH), BF16)],
        compiler_params=pltpu.CompilerParams(
            dimension_semantics=("arbitrary",), vmem_limit_bytes=VMEM_LIMIT),
        name="mem_kv",
    )(mem, mem_g, w_mem_kv, kg, ones)


def _mix_out_kernel(x_ref, gates_ref, feats_ref, sb_ref, mk_ref, mv_ref, cw_ref, cb_ref,
                    wg_ref, bg_ref, lam_ref, xq_g_ref, ones_ref, wo32_ref, o_ref, ext_ref,
                    h_ref, wo_ref):
    tm = x_ref.shape[0]
    _cast_weight_once(wo32_ref, wo_ref, (pl.program_id(0) == 0) & (pl.program_id(1) == 0))

    @pl.when(pl.program_id(1) == 0)
    def _():
        ext_ref[0:SUBLANES, :] = jnp.zeros((SUBLANES, LRU_WIDTH), F32)
        h_ref[...] = jnp.zeros_like(h_ref)

    lru_x = feats_ref[:, 0:LRU_WIDTH]
    xa_q = feats_ref[:, LRU_WIDTH:]

    ext_ref[SUBLANES:SUBLANES + tm, :] = lru_x
    xc = cb_ref[...] + cw_ref[CONV_WIDTH - 1:CONV_WIDTH, :] * lru_x
    for tap in range(CONV_WIDTH - 1):
        shift = CONV_WIDTH - 1 - tap
        xc = xc + cw_ref[tap:tap + 1, :] * ext_ref[SUBLANES - shift:SUBLANES - shift + tm, :]
    ext_ref[0:SUBLANES, :] = ext_ref[tm:tm + SUBLANES, :]

    gates = _dot(xc.astype(BF16), wg_ref[...]) + bg_ref[...]
    r = _sigmoid(gates[:, :LRU_WIDTH])
    i_gate = _sigmoid(gates[:, LRU_WIDTH:])
    neg_lam = -lam_ref[...]
    softplus_neg_lam = jnp.maximum(neg_lam, 0.0) + jnp.log(1.0 + jnp.exp(-jnp.abs(neg_lam)))
    log_a = (-LRU_C) * r * softplus_neg_lam
    a = jnp.exp(log_a)
    one_minus_a2 = 1.0 - jnp.exp(2.0 * log_a)
    root = jnp.where(one_minus_a2 > 0.0, one_minus_a2 * lax.rsqrt(one_minus_a2), 0.0)
    u = root * (i_gate * xc)
    n_groups = tm // SUBLANES
    a = a.reshape(n_groups, SUBLANES, LRU_WIDTH)
    u = u.reshape(n_groups, SUBLANES, LRU_WIDTH)
    sub = lax.broadcasted_iota(jnp.int32, (1, SUBLANES, LRU_WIDTH), 1)
    d = 1
    while d < SUBLANES:
        keep = sub >= d
        a_prev = jnp.where(keep, pltpu.roll(a, d, 1), 1.0)
        u_prev = jnp.where(keep, pltpu.roll(u, d, 1), 0.0)
        u = u + a * u_prev
        a = a * a_prev
        d *= 2
    h_in = h_ref[...]
    h_before = []
    for g in range(n_groups):
        h_before.append(h_in)
        a_tot = jnp.broadcast_to(a[g, SUBLANES - 1:SUBLANES, :], (SUBLANES, LRU_WIDTH))
        u_tot = jnp.broadcast_to(u[g, SUBLANES - 1:SUBLANES, :], (SUBLANES, LRU_WIDTH))
        h_in = a_tot * h_in + u_tot
    h_ref[...] = h_in
    h = (u + a * jnp.stack(h_before, axis=0)).reshape(tm, LRU_WIDTH)

    lane = lax.broadcasted_iota(jnp.int32, (1, XA_WIDTH), 1)
    qn = _head_rms(xa_q, ones_ref[...]) * xq_g_ref[...]
    mk = mk_ref[...]
    mv = mv_ref[...]
    xa = jnp.zeros((tm, XA_WIDTH), F32)
    for head in range(XA_WIDTH // HEAD_DIM):
        in_head = (lane >= head * HEAD_DIM) & (lane < (head + 1) * HEAD_DIM)
        qh = jnp.where(in_head, qn, 0.0).astype(BF16)
        s = _dot_nt(qh, mk)
        p = jnp.exp2(s - jnp.max(s, axis=-1, keepdims=True))
        denom = jnp.sum(p, axis=-1, keepdims=True)
        oh = _dot(p.astype(BF16), mv)
        xa = jnp.where(in_head, oh / denom, xa)

    y = jnp.concatenate([sb_ref[...], h, xa], axis=1).astype(BF16) * gates_ref[...]
    o_ref[...] = x_ref[...] + _dot(y, wo_ref[...])


def _mix_out(x, gates, feats, sb, mk, mv, conv_w, conv_b, w_gates, b_gates, lam, xq_g, ones,
             w_out, layer):
    b, s, _ = x.shape
    tile = lambda w: pl.BlockSpec((None, ROW_TILE, w), lambda bi, si: (bi, si, 0))
    per_b = lambda r, w: pl.BlockSpec((None, r, w), lambda bi, si: (bi, 0, 0))
    const = lambda r, w: pl.BlockSpec((r, w), lambda bi, si: (0, 0))
    return pl.pallas_call(
        _mix_out_kernel,
        grid=(b, s // ROW_TILE),
        in_specs=[
            tile(D_MODEL), tile(D_MIX), tile(LRU_WIDTH + XA_WIDTH), tile(SB_WIDTH),
            per_b(N_MEM, XA_WIDTH), per_b(N_MEM, XA_WIDTH),
            const(CONV_WIDTH, LRU_WIDTH), const(1, LRU_WIDTH),
            const(LRU_WIDTH, 2 * LRU_WIDTH), const(1, 2 * LRU_WIDTH),
            const(1, LRU_WIDTH), const(1, XA_WIDTH),
            const(MXU_WIDTH, MXU_WIDTH),
            pl.BlockSpec((None, D_MIX, D_MODEL), lambda bi, si: (layer, 0, 0),
                         pipeline_mode=pl.Buffered(1)),
        ],
        out_specs=tile(D_MODEL),
        out_shape=jax.ShapeDtypeStruct((b, s, D_MODEL), F32),
        scratch_shapes=[pltpu.VMEM((ROW_TILE + SUBLANES, LRU_WIDTH), F32),
                        pltpu.VMEM((SUBLANES, LRU_WIDTH), F32),
                        pltpu.VMEM((D_MIX, D_MODEL), BF16)],
        compiler_params=pltpu.CompilerParams(
            dimension_semantics=("arbitrary", "arbitrary"), vmem_limit_bytes=VMEM_LIMIT),
        name="mix_out",
    )(x, gates, feats, sb, mk, mv, conv_w, conv_b, w_gates, b_gates, lam, xq_g, ones, w_out)


def _block_diag(w):
    n, d, _ = w.shape
    eye = jnp.eye(n, dtype=w.dtype)
    return (eye[:, None, :, None] * w[:, :, None, :]).reshape(n * d, n * d)


def kernel(x, mem, norm_g, w_in, sb_q_g, sb_k_g, conv_w, conv_b, w_rg, b_rg, w_ig, b_ig,
           lru_lambda, xa_q_g, xa_k_g, mem_g, w_mem_kv, w_out):
    b, s, d = x.shape
    depth = norm_g.shape[0]
    scale = HEAD_DIM ** -0.5
    idx = jnp.arange(MXU_WIDTH)
    group_ones = (idx[:, None] // HEAD_DIM == idx[None, :] // HEAD_DIM).astype(BF16)
    upper = (idx[:, None] > idx[None, :]).astype(BF16)

    for l in range(depth):
        qg = (jnp.tile(sb_q_g[l], SB_WIDTH // HEAD_DIM) * (scale * LOG2_E)).reshape(1, SB_WIDTH)
        kg = jnp.tile(sb_k_g[l], SB_WIDTH // HEAD_DIM).reshape(1, SB_WIDTH)
        q, k, v, gates, feats = _in_proj(x.reshape(b * s, d), norm_g[l].reshape(1, d),
                                         w_in, l, qg, kg, group_ones)
        sb = _sb_attn(q.reshape(b, s, SB_WIDTH), k.reshape(b, s, SB_WIDTH),
                      v.reshape(b, s, SB_WIDTH), upper)
        mkg = jnp.tile(xa_k_g[l], XA_WIDTH // HEAD_DIM).reshape(1, XA_WIDTH)
        mk, mv = _mem_kv(mem, mem_g[l].reshape(1, d), w_mem_kv, l, mkg, group_ones)
        w_gates = jnp.concatenate([_block_diag(w_rg[l]), _block_diag(w_ig[l])],
                                  axis=1).astype(BF16)
        b_gates = jnp.concatenate([b_rg[l], b_ig[l]]).reshape(1, 2 * LRU_WIDTH)
        xq_g = (jnp.tile(xa_q_g[l], XA_WIDTH // HEAD_DIM) * (scale * LOG2_E)).reshape(1, XA_WIDTH)
        x = _mix_out(x, gates.reshape(b, s, D_MIX), feats.reshape(b, s, LRU_WIDTH + XA_WIDTH),
                     sb, mk, mv, conv_w[l], conv_b[l].reshape(1, LRU_WIDTH), w_gates, b_gates,
                     lru_lambda[l].reshape(1, LRU_WIDTH), xq_g, group_ones, w_out, l)
    return x
```

```python
import math

import jax
import jax.numpy as jnp
from jax import lax
from jax.experimental import pallas as pl
from jax.experimental.pallas import tpu as pltpu

D_MODEL = 1024
HEAD_DIM = 64
SB_WIDTH = 512
LRU_WIDTH = 256
XA_WIDTH = 256
N_MEM = 256
CONV_WIDTH = 4
LRU_C = 8.0
EPS = 1e-6
D_IN = 4 * SB_WIDTH + 2 * LRU_WIDTH + 2 * XA_WIDTH
D_MIX = SB_WIDTH + LRU_WIDTH + XA_WIDTH

MXU_WIDTH = 256
LANES = 128
SUBLANES = 8
ROW_TILE = 1024
ATT_TILE = MXU_WIDTH
ATT_QUAD = ATT_TILE // 2
ATT_WINDOW = 2 * ATT_TILE
VMEM_LIMIT = 48 * 1024 * 1024

LOG2_E = math.log2(math.e)
F32_EXP2_ZERO = -150.0
MASKED_LOG2 = -1e30

F32 = jnp.float32
BF16 = jnp.bfloat16


def _dot(a, b):
    return jnp.dot(a, b, preferred_element_type=F32)


def _dot_nt(a, b):
    return lax.dot_general(a, b, (((1,), (1,)), ((), ())), preferred_element_type=F32)


def _head_rms(t, group_ones):
    outs = []
    for j in range(t.shape[1] // MXU_WIDTH):
        tj = t[:, j * MXU_WIDTH:(j + 1) * MXU_WIDTH]
        ss = _dot((tj * tj).astype(BF16), group_ones)
        outs.append(tj * lax.rsqrt(ss * (1.0 / HEAD_DIM) + EPS))
    return outs[0] if len(outs) == 1 else jnp.concatenate(outs, axis=1)


def _sigmoid(x):
    return 0.5 * jnp.tanh(0.5 * x) + 0.5


def _silu(x):
    half = 0.5 * x
    return half * jnp.tanh(half) + half


def _cast_weight_once(w_ref, wb_ref, first_step):
    @pl.when(first_step)
    def _():
        for c in range(0, w_ref.shape[1], 2 * MXU_WIDTH):
            wb_ref[:, c:c + 2 * MXU_WIDTH] = w_ref[:, c:c + 2 * MXU_WIDTH].astype(BF16)


def _in_proj_kernel(x_ref, g_ref, w_ref, qg_ref, kg_ref, ones_ref,
                    q_ref, k_ref, v_ref, gates_ref, feats_ref, wb_ref):
    _cast_weight_once(w_ref, wb_ref, pl.program_id(0) == 0)
    x = x_ref[...]
    ms = jnp.mean(x * x, axis=-1, keepdims=True)
    h = (x * lax.rsqrt(ms + EPS) * g_ref[...]).astype(BF16)
    ones = ones_ref[...]

    def proj(c0, width):
        return _dot(h, wb_ref[:, c0:c0 + width])

    q_ref[...] = (_head_rms(proj(0, SB_WIDTH), ones) * qg_ref[...]).astype(BF16)
    k_ref[...] = (_head_rms(proj(SB_WIDTH, SB_WIDTH), ones) * kg_ref[...]).astype(BF16)
    v_ref[...] = proj(2 * SB_WIDTH, SB_WIDTH).astype(BF16)
    gates_ref[:, 0:SB_WIDTH] = _silu(proj(3 * SB_WIDTH, SB_WIDTH)).astype(BF16)
    c0 = 4 * SB_WIDTH
    for g, width in enumerate((LRU_WIDTH, XA_WIDTH)):
        pair = proj(c0, 2 * width)
        feats_ref[:, g * LRU_WIDTH:g * LRU_WIDTH + width] = pair[:, :width]
        gates_ref[:, SB_WIDTH + g * LRU_WIDTH:SB_WIDTH + g * LRU_WIDTH + width] = (
            _silu(pair[:, width:]).astype(BF16))
        c0 += 2 * width


def _in_proj(x2d, norm_g, w_in, layer, qg, kg, ones):
    m = x2d.shape[0]
    row = lambda i: (i, 0)
    const = lambda i: (0, 0)
    return pl.pallas_call(
        _in_proj_kernel,
        grid=(m // ROW_TILE,),
        in_specs=[
            pl.BlockSpec((ROW_TILE, D_MODEL), row),
            pl.BlockSpec((1, D_MODEL), const),
            pl.BlockSpec((None, D_MODEL, D_IN), lambda i: (layer, 0, 0),
                         pipeline_mode=pl.Buffered(1)),
            pl.BlockSpec((1, SB_WIDTH), const),
            pl.BlockSpec((1, SB_WIDTH), const),
            pl.BlockSpec((MXU_WIDTH, MXU_WIDTH), const),
        ],
        out_specs=[
            pl.BlockSpec((ROW_TILE, SB_WIDTH), row),
            pl.BlockSpec((ROW_TILE, SB_WIDTH), row),
            pl.BlockSpec((ROW_TILE, SB_WIDTH), row),
            pl.BlockSpec((ROW_TILE, D_MIX), row),
            pl.BlockSpec((ROW_TILE, LRU_WIDTH + XA_WIDTH), row),
        ],
        out_shape=[
            jax.ShapeDtypeStruct((m, SB_WIDTH), BF16),
            jax.ShapeDtypeStruct((m, SB_WIDTH), BF16),
            jax.ShapeDtypeStruct((m, SB_WIDTH), BF16),
            jax.ShapeDtypeStruct((m, D_MIX), BF16),
            jax.ShapeDtypeStruct((m, LRU_WIDTH + XA_WIDTH), F32),
        ],
        scratch_shapes=[pltpu.VMEM((D_MODEL, D_IN), BF16)],
        compiler_params=pltpu.CompilerParams(
            dimension_semantics=("arbitrary",), vmem_limit_bytes=VMEM_LIMIT),
        name="in_proj",
    )(x2d, norm_g, w_in, qg, kg, ones)


def _log2_terms(z):
    t = jnp.log2(1.0 + jnp.exp2(-jnp.abs(z)))
    log_sig = jnp.minimum(z, 0.0) - t
    return log_sig, log_sig - z


def _sb_attn_kernel(q_ref, k_ref, v_ref, upper_ref, o_ref, *scratch):
    seq = q_ref.shape[0]
    n_blocks = seq // ATT_TILE
    assert n_blocks >= 3
    heads = range(LANES // HEAD_DIM)
    sets = (scratch[0:3], scratch[3:6])
    carry_ref, cmax_ref = scratch[6:8]
    lane = lax.broadcasted_iota(jnp.int32, (1, LANES), 1)
    causal = (lax.broadcasted_iota(jnp.int32, (ATT_QUAD, ATT_QUAD), 1)
              < lax.broadcasted_iota(jnp.int32, (ATT_QUAD, ATT_QUAD), 0))

    def tile_start(j):
        if isinstance(j, int):
            return j * ATT_TILE
        return pl.multiple_of(j * ATT_TILE, ATT_TILE)

    def masked_queries(i):
        q = q_ref[pl.ds(tile_start(i), ATT_TILE), :]
        return [jnp.where((lane >= h * HEAD_DIM) & (lane < (h + 1) * HEAD_DIM), q,
                          jnp.zeros_like(q)) for h in heads]

    def window_start(i):
        return tile_start(max(i - 1, 0) if isinstance(i, int) else jnp.maximum(i - 1, 0))

    def block_rows(i):
        return pl.ds(tile_start(i), ATT_TILE)

    def put(bufs, h, tile, r0, c0, log_sig, log_fail):
        lf_ref, ls_ref, _ = bufs
        nr, nc = log_sig.shape
        lf_row = (2 * h + tile) * ATT_TILE + r0
        lf_ref[lf_row:lf_row + nr, c0:c0 + nc] = log_fail.astype(BF16)
        ls_ref[h * ATT_TILE + r0:h * ATT_TILE + r0 + nr,
               tile * ATT_TILE + c0:tile * ATT_TILE + c0 + nc] = log_sig
        return jnp.sum(log_fail, axis=-1, keepdims=True)

    def put_masked(bufs, h, tile, r0, c0, nr, nc):
        put(bufs, h, tile, r0, c0, jnp.full((nr, nc), MASKED_LOG2, F32), jnp.zeros((nr, nc), F32))

    def full_tile(bufs, h, tile, z):
        return put(bufs, h, tile, 0, 0, *_log2_terms(z))

    def diagonal_tile(bufs, h, tile, z):
        def masked(log_sig, log_fail):
            return jnp.where(causal, log_sig, MASKED_LOG2), jnp.where(causal, log_fail, 0.0)

        top = put(bufs, h, tile, 0, 0, *masked(*_log2_terms(z[:ATT_QUAD, :ATT_QUAD])))
        bottom = put(bufs, h, tile, ATT_QUAD, 0, *_log2_terms(z[ATT_QUAD:, :ATT_QUAD]))
        bottom = bottom + put(bufs, h, tile, ATT_QUAD, ATT_QUAD,
                              *masked(*_log2_terms(z[ATT_QUAD:, ATT_QUAD:])))
        return jnp.concatenate([top, bottom], axis=0)

    def scores(i, bufs, first=False):
        rs_ref = bufs[2]
        kk = k_ref[pl.ds(window_start(i), ATT_WINDOW), :]
        z = _dot_nt(jnp.concatenate(masked_queries(i), axis=0), kk)
        cmax = None
        for h in heads:
            zh = z[h * ATT_TILE:(h + 1) * ATT_TILE]
            if first:
                put_masked(bufs, h, 0, 0, ATT_QUAD, ATT_QUAD, ATT_QUAD)
                rs_old = diagonal_tile(bufs, h, 0, zh[:, :ATT_TILE])
                put_masked(bufs, h, 1, 0, 0, ATT_TILE, ATT_TILE)
                rs_new = jnp.zeros((ATT_TILE, 1), F32)
            else:
                rs_old = full_tile(bufs, h, 0, zh[:, :ATT_TILE])
                rs_new = diagonal_tile(bufs, h, 1, zh[:, ATT_TILE:])
            rs_ref[h] = rs_new
            carry = rs_old + rs_new
            carry_ref[i, h] = carry
            m = jnp.max(carry)
            cmax = m if cmax is None else jnp.maximum(cmax, m)
        cmax_ref[i] = cmax

    def weights(i, bufs):
        lf_ref, ls_ref, rs_ref = bufs
        vv = v_ref[pl.ds(window_start(i), ATT_WINDOW), :]
        later = _dot(lf_ref[...], upper_ref[...])
        ws = []
        for h in heads:
            q_rows = slice(h * ATT_TILE, (h + 1) * ATT_TILE)
            old = slice(2 * h * ATT_TILE, (2 * h + 1) * ATT_TILE)
            new = slice((2 * h + 1) * ATT_TILE, (2 * h + 2) * ATT_TILE)
            x_old = ls_ref[q_rows, 0:ATT_TILE] + (later[old] + rs_ref[h])
            x_new = ls_ref[q_rows, ATT_TILE:ATT_WINDOW] + later[new]
            ws.append(jnp.concatenate([jnp.exp2(x_old), jnp.exp2(x_new)], axis=1))
        pv = _dot(jnp.concatenate(ws, axis=0).astype(BF16), vv)
        o_ref[block_rows(i), :] = jnp.where(lane < HEAD_DIM, pv[:ATT_TILE], pv[ATT_TILE:])

    def finish(i, _):
        qms = masked_queries(i)

        def more(state):
            j, _, cmax = state
            return jnp.logical_and(j >= 0, cmax > F32_EXP2_ZERO)

        def sweep(state):
            j, carries, _ = state
            c0 = pl.multiple_of(j * ATT_TILE, ATT_TILE)
            kt = k_ref[pl.ds(c0, ATT_TILE), :]
            vt = v_ref[pl.ds(c0, ATT_TILE), :]
            new, pvs = [], []
            for h in heads:
                log_sig, log_fail = _log2_terms(_dot_nt(qms[h], kt))
                later = _dot(log_fail.astype(BF16), upper_ref[...])
                w = jnp.exp2(log_sig + later + carries[h])
                pvs.append(_dot(w.astype(BF16), vt))
                new.append(carries[h] + jnp.sum(log_fail, axis=-1, keepdims=True))
            o_ref[block_rows(i), :] += jnp.where(lane < HEAD_DIM, pvs[0], pvs[1])
            return j - 1, tuple(new), jnp.max(jnp.maximum(new[0], new[1]))

        carries = tuple(carry_ref[i, h] for h in heads)
        lax.while_loop(more, sweep, (jnp.maximum(i - 1, 0) - 1, carries, cmax_ref[i]))
        return 0

    for h in heads:
        put_masked(sets[1], h, 1, 0, ATT_QUAD, ATT_QUAD, ATT_QUAD)
    scores(0, sets[0], first=True)
    for i in range(1, n_blocks):
        scores(i, sets[i % 2])
        weights(i - 1, sets[(i - 1) % 2])
    weights(n_blocks - 1, sets[(n_blocks - 1) % 2])

    worst = cmax_ref[2]
    for i in range(3, n_blocks):
        worst = jnp.maximum(worst, cmax_ref[i])

    @pl.when(worst > F32_EXP2_ZERO)
    def _():
        lax.fori_loop(2, n_blocks, finish, 0)


def _sb_attn(q, k, v, upper):
    b, s, _ = q.shape
    n_heads = LANES // HEAD_DIM
    blk = pl.BlockSpec((None, s, LANES), lambda bi, hp: (bi, 0, hp))
    return pl.pallas_call(
        _sb_attn_kernel,
        grid=(b, SB_WIDTH // LANES),
        in_specs=[blk, blk, blk,
                  pl.BlockSpec((ATT_TILE, ATT_TILE), lambda bi, hp: (0, 0))],
        out_specs=blk,
        out_shape=jax.ShapeDtypeStruct((b, s, SB_WIDTH), F32),
        scratch_shapes=2 * [
            pltpu.VMEM((n_heads * ATT_WINDOW, ATT_TILE), BF16),
            pltpu.VMEM((n_heads * ATT_TILE, ATT_WINDOW), F32),
            pltpu.VMEM((n_heads, ATT_TILE, 1), F32),
        ] + [
            pltpu.VMEM((s // ATT_TILE, n_heads, ATT_TILE, 1), F32),
            pltpu.SMEM((s // ATT_TILE,), F32),
        ],
        compiler_params=pltpu.CompilerParams(
            dimension_semantics=("arbitrary", "arbitrary"), vmem_limit_bytes=VMEM_LIMIT),
        name="sb_attn",
    )(q, k, v, upper)


def _mem_kv_kernel(mem_ref, g_ref, w_ref, kg_ref, ones_ref, mk_ref, mv_ref):
    x = mem_ref[...]
    ms = jnp.mean(x * x, axis=-1, keepdims=True)
    h = (x * lax.rsqrt(ms + EPS) * g_ref[...]).astype(BF16)
    kv = _dot(h, w_ref[...].astype(BF16))
    mk = _head_rms(kv[:, :XA_WIDTH], ones_ref[...]) * kg_ref[...]
    mk_ref[...] = mk.astype(BF16)
    mv_ref[...] = kv[:, XA_WIDTH:].astype(BF16)


def _mix_out_kernel(x_ref, gates_ref, feats_ref, sb_ref, mem_ref, mem_g_ref, wkv_ref, kg_ref,
                    cw_ref, cb_ref, wg_ref, bg_ref, lam_ref, xq_g_ref, ones_ref, wo32_ref,
                    o_ref, ext_ref, h_ref, wo_ref, mk_ref, mv_ref):
    tm = x_ref.shape[0]
    _cast_weight_once(wo32_ref, wo_ref, (pl.program_id(0) == 0) & (pl.program_id(1) == 0))

    @pl.when(pl.program_id(1) == 0)
    def _():
        ext_ref[0:SUBLANES, :] = jnp.zeros((SUBLANES, LRU_WIDTH), F32)
        h_ref[...] = jnp.zeros_like(h_ref)
        _mem_kv_kernel(mem_ref, mem_g_ref, wkv_ref, kg_ref, ones_ref, mk_ref, mv_ref)

    lru_x = feats_ref[:, 0:LRU_WIDTH]
    xa_q = feats_ref[:, LRU_WIDTH:]

    ext_ref[SUBLANES:SUBLANES + tm, :] = lru_x
    xc = cb_ref[...] + cw_ref[CONV_WIDTH - 1:CONV_WIDTH, :] * lru_x
    for tap in range(CONV_WIDTH - 1):
        shift = CONV_WIDTH - 1 - tap
        xc = xc + cw_ref[tap:tap + 1, :] * ext_ref[SUBLANES - shift:SUBLANES - shift + tm, :]
    ext_ref[0:SUBLANES, :] = ext_ref[tm:tm + SUBLANES, :]

    gates = _dot(xc.astype(BF16), wg_ref[...]) + bg_ref[...]
    r = _sigmoid(gates[:, :LRU_WIDTH])
    i_gate = _sigmoid(gates[:, LRU_WIDTH:])
    neg_lam = -lam_ref[...]
    softplus_neg_lam = jnp.maximum(neg_lam, 0.0) + jnp.log(1.0 + jnp.exp(-jnp.abs(neg_lam)))
    log_a = (-LRU_C) * r * softplus_neg_lam
    a = jnp.exp(log_a)
    one_minus_a2 = 1.0 - jnp.exp(2.0 * log_a)
    root = jnp.where(one_minus_a2 > 0.0, one_minus_a2 * lax.rsqrt(one_minus_a2), 0.0)
    u = root * (i_gate * xc)
    n_groups = tm // SUBLANES
    a = a.reshape(n_groups, SUBLANES, LRU_WIDTH)
    u = u.reshape(n_groups, SUBLANES, LRU_WIDTH)
    sub = lax.broadcasted_iota(jnp.int32, (1, SUBLANES, LRU_WIDTH), 1)
    d = 1
    while d < SUBLANES:
        keep = sub >= d
        a_prev = jnp.where(keep, pltpu.roll(a, d, 1), 1.0)
        u_prev = jnp.where(keep, pltpu.roll(u, d, 1), 0.0)
        u = u + a * u_prev
        a = a * a_prev
        d *= 2
    h_in = h_ref[...]
    h_before = []
    for g in range(n_groups):
        h_before.append(h_in)
        a_tot = jnp.broadcast_to(a[g, SUBLANES - 1:SUBLANES, :], (SUBLANES, LRU_WIDTH))
        u_tot = jnp.broadcast_to(u[g, SUBLANES - 1:SUBLANES, :], (SUBLANES, LRU_WIDTH))
        h_in = a_tot * h_in + u_tot
    h_ref[...] = h_in
    h = (u + a * jnp.stack(h_before, axis=0)).reshape(tm, LRU_WIDTH)

    lane = lax.broadcasted_iota(jnp.int32, (1, XA_WIDTH), 1)
    qn = _head_rms(xa_q, ones_ref[...]) * xq_g_ref[...]
    mk = mk_ref[...]
    mv = mv_ref[...]
    xa = jnp.zeros((tm, XA_WIDTH), F32)
    for head in range(XA_WIDTH // HEAD_DIM):
        in_head = (lane >= head * HEAD_DIM) & (lane < (head + 1) * HEAD_DIM)
        qh = jnp.where(in_head, qn, 0.0).astype(BF16)
        s = _dot_nt(qh, mk)
        p = jnp.exp2(s - jnp.max(s, axis=-1, keepdims=True))
        denom = jnp.sum(p, axis=-1, keepdims=True)
        oh = _dot(p.astype(BF16), mv)
        xa = jnp.where(in_head, oh / denom, xa)

    y = jnp.concatenate([sb_ref[...], h, xa], axis=1).astype(BF16) * gates_ref[...]
    o_ref[...] = x_ref[...] + _dot(y, wo_ref[...])


def _mix_out(x, gates, feats, sb, mem, mem_g, w_mem_kv, kg, conv_w, conv_b, w_gates, b_gates,
             lam, xq_g, ones, w_out, layer):
    b, s, _ = x.shape
    tile = lambda w: pl.BlockSpec((None, ROW_TILE, w), lambda bi, si: (bi, si, 0))
    per_b = lambda r, w: pl.BlockSpec((None, r, w), lambda bi, si: (bi, 0, 0))
    const = lambda r, w: pl.BlockSpec((r, w), lambda bi, si: (0, 0))
    return pl.pallas_call(
        _mix_out_kernel,
        grid=(b, s // ROW_TILE),
        in_specs=[
            tile(D_MODEL), tile(D_MIX), tile(LRU_WIDTH + XA_WIDTH), tile(SB_WIDTH),
            per_b(N_MEM, D_MODEL), const(1, D_MODEL),
            pl.BlockSpec((None, D_MODEL, 2 * XA_WIDTH), lambda bi, si: (layer, 0, 0),
                         pipeline_mode=pl.Buffered(1)),
            const(1, XA_WIDTH),
            const(CONV_WIDTH, LRU_WIDTH), const(1, LRU_WIDTH),
            const(LRU_WIDTH, 2 * LRU_WIDTH), const(1, 2 * LRU_WIDTH),
            const(1, LRU_WIDTH), const(1, XA_WIDTH),
            const(MXU_WIDTH, MXU_WIDTH),
            pl.BlockSpec((None, D_MIX, D_MODEL), lambda bi, si: (layer, 0, 0),
                         pipeline_mode=pl.Buffered(1)),
        ],
        out_specs=tile(D_MODEL),
        out_shape=jax.ShapeDtypeStruct((b, s, D_MODEL), F32),
        scratch_shapes=[pltpu.VMEM((ROW_TILE + SUBLANES, LRU_WIDTH), F32),
                        pltpu.VMEM((SUBLANES, LRU_WIDTH), F32),
                        pltpu.VMEM((D_MIX, D_MODEL), BF16),
                        pltpu.VMEM((N_MEM, XA_WIDTH), BF16),
                        pltpu.VMEM((N_MEM, XA_WIDTH), BF16)],
        compiler_params=pltpu.CompilerParams(
            dimension_semantics=("arbitrary", "arbitrary"), vmem_limit_bytes=VMEM_LIMIT),
        name="mix_out",
    )(x, gates, feats, sb, mem, mem_g, w_mem_kv, kg, conv_w, conv_b, w_gates, b_gates, lam,
      xq_g, ones, w_out)


def _block_diag(w):
    n, d, _ = w.shape
    eye = jnp.eye(n, dtype=w.dtype)
    return (eye[:, None, :, None] * w[:, :, None, :]).reshape(n * d, n * d)


def kernel(x, mem, norm_g, w_in, sb_q_g, sb_k_g, conv_w, conv_b, w_rg, b_rg, w_ig, b_ig,
           lru_lambda, xa_q_g, xa_k_g, mem_g, w_mem_kv, w_out):
    b, s, d = x.shape
    depth = norm_g.shape[0]
    scale = HEAD_DIM ** -0.5
    idx = jnp.arange(MXU_WIDTH)
    group_ones = (idx[:, None] // HEAD_DIM == idx[None, :] // HEAD_DIM).astype(BF16)
    upper = (idx[:, None] > idx[None, :]).astype(BF16)

    for l in range(depth):
        qg = (jnp.tile(sb_q_g[l], SB_WIDTH // HEAD_DIM) * (scale * LOG2_E)).reshape(1, SB_WIDTH)
        kg = jnp.tile(sb_k_g[l], SB_WIDTH // HEAD_DIM).reshape(1, SB_WIDTH)
        q, k, v, gates, feats = _in_proj(x.reshape(b * s, d), norm_g[l].reshape(1, d),
                                         w_in, l, qg, kg, group_ones)
        sb = _sb_attn(q.reshape(b, s, SB_WIDTH), k.reshape(b, s, SB_WIDTH),
                      v.reshape(b, s, SB_WIDTH), upper)
        mkg = jnp.tile(xa_k_g[l], XA_WIDTH // HEAD_DIM).reshape(1, XA_WIDTH)
        w_gates = jnp.concatenate([_block_diag(w_rg[l]), _block_diag(w_ig[l])],
                                  axis=1).astype(BF16)
        b_gates = jnp.concatenate([b_rg[l], b_ig[l]]).reshape(1, 2 * LRU_WIDTH)
        xq_g = (jnp.tile(xa_q_g[l], XA_WIDTH // HEAD_DIM) * (scale * LOG2_E)).reshape(1, XA_WIDTH)
        x = _mix_out(x, gates.reshape(b, s, D_MIX), feats.reshape(b, s, LRU_WIDTH + XA_WIDTH),
                     sb, mem, mem_g[l].reshape(1, d), w_mem_kv, mkg, conv_w[l],
                     conv_b[l].reshape(1, LRU_WIDTH), w_gates, b_gates,
                     lru_lambda[l].reshape(1, LRU_WIDTH), xq_g, group_ones, w_out, l)
    return x
```
